```python
import math
import jax, jax.numpy as jnp
from jax import lax
import numpy as np

D_MODEL = 1024
BATCH = 16
SEQ = 2048
DEPTH = 2

D_RNN = D_MODEL
RNN_BLOCKS = 8
RNN_BLOCK_W = D_RNN // RNN_BLOCKS
CONV_RNN = 4
RG_LRU_C = 8.0
DIFF_HEADS = 4
DIFF_HEAD_DIM = 64
DIFF_QK = DIFF_HEADS * 2 * DIFF_HEAD_DIM
DIFF_WIDTH = DIFF_HEADS * 2 * DIFF_HEAD_DIM
FOX_HEADS = 8
FOX_HEAD_DIM = 64
FOX_WIDTH = FOX_HEADS * FOX_HEAD_DIM
N_BUCKETS = 32
MAX_EXACT = N_BUCKETS // 2
MAX_DISTANCE = 128
Q_BLOCK = 128
D_FF = ((8 * D_MODEL // 3 + 127) // 128) * 128
CONV_FFN = 3
N_BRANCH = 3
EPS = 1e-6

IN_WIDTHS = (D_RNN, D_RNN, DIFF_QK, DIFF_QK, DIFF_WIDTH, FOX_WIDTH, FOX_WIDTH, FOX_WIDTH, FOX_HEADS, N_BRANCH * D_MODEL)
N_IN = int(sum(IN_WIDTHS))
IN_SPLITS = tuple(int(v) for v in np.cumsum(IN_WIDTHS)[:-1])

kernel_name = "hybrid_rglru_diffattn_fox_gated_trunk"


def rmsnorm(x, g):
    xf = x.astype(jnp.float32)
    y = xf * lax.rsqrt(jnp.mean(xf * xf, axis=-1, keepdims=True) + EPS)
    return (y * g.astype(jnp.float32)).astype(x.dtype)


def causal_dwconv(x, w, b):
    k = w.shape[0]
    s = x.shape[1]
    xp = jnp.pad(x, ((0, 0), (k - 1, 0), (0, 0)))
    y = b + xp[:, 0:s] * w[0]
    for j in range(1, k):
        y = y + xp[:, j:j + s] * w[j]
    return y


def _to_blocks(t):
    b, s = t.shape[:2]
    return jnp.moveaxis(t.reshape(b, s // Q_BLOCK, Q_BLOCK, *t.shape[2:]), 1, 0)


def _from_blocks(t):
    nb, b = t.shape[:2]
    t = jnp.moveaxis(t, 0, 1)
    return t.reshape(b, nb * Q_BLOCK, *t.shape[3:])


def t5_bucket(dist):
    n = jnp.maximum(dist, 0)
    nf = jnp.maximum(n, 1).astype(jnp.float32)
    large = MAX_EXACT + (jnp.log(nf / MAX_EXACT) / math.log(MAX_DISTANCE / MAX_EXACT)
                         * (N_BUCKETS - MAX_EXACT)).astype(jnp.int32)
    large = jnp.minimum(large, N_BUCKETS - 1)
    return jnp.where(n < MAX_EXACT, n, large)


def rg_lru(x, w_r, b_r, w_i, b_i, a_param):
    bsz, s, c = x.shape
    xb = x.reshape(bsz, s, RNN_BLOCKS, RNN_BLOCK_W)
    r = jax.nn.sigmoid(jnp.einsum('bsnc,ncd->bsnd', xb, w_r).reshape(bsz, s, c) + b_r)
    i = jax.nn.sigmoid(jnp.einsum('bsnc,ncd->bsnd', xb, w_i).reshape(bsz, s, c) + b_i)
    log_a = (-RG_LRU_C * r.astype(jnp.float32)) * jax.nn.softplus(-a_param.astype(jnp.float32))
    a = jnp.exp(log_a)
    u = jnp.sqrt(-jnp.expm1(2.0 * log_a)) * (i * x).astype(jnp.float32)

    def combine(left, right):
        a1, b1 = left
        a2, b2 = right
        return a1 * a2, a2 * b1 + b2

    _, h = lax.associative_scan(combine, (a, u), axis=1)
    return h.astype(x.dtype)


def diff_attention(q, k, v, lam, rel_table):
    s_len = q.shape[1]
    scale = DIFF_HEAD_DIM ** -0.5
    k_pos = jnp.arange(s_len)

    def one_block(args):
        qb, blk = args
        q_pos = blk * Q_BLOCK + jnp.arange(Q_BLOCK)
        dist = q_pos[:, None] - k_pos[None, :]
        bias = jnp.moveaxis(rel_table[t5_bucket(dist)], -1, 0)
        sc = jnp.einsum('bqhcd,bkhcd->bchqk', qb, k).astype(jnp.float32) * scale + bias.astype(jnp.float32)
        sc = jnp.where(dist >= 0, sc, -jnp.inf)
        p = jax.nn.softmax(sc, axis=-1)
        w = p[:, 0] - lam * p[:, 1]
        return jnp.einsum('bhqk,bkhd->bqhd', w.astype(v.dtype), v)

    out = lax.map(one_block, (_to_blocks(q), jnp.arange(s_len // Q_BLOCK)))
    return _from_blocks(out)


def forgetting_attention(q, k, v, log_f):
    s_len = q.shape[1]
    scale = FOX_HEAD_DIM ** -0.5
    k_pos = jnp.arange(s_len)
    c = jnp.cumsum(log_f.astype(jnp.float32), axis=1)
    ck = jnp.moveaxis(c, -1, 1)

    def one_block(args):
        qb, cq, blk = args
        q_pos = blk * Q_BLOCK + jnp.arange(Q_BLOCK)
        dist = q_pos[:, None] - k_pos[None, :]
        decay = jnp.moveaxis(cq, -1, 1)[..., None] - ck[:, :, None, :]
        sc = jnp.einsum('bqhd,bkhd->bhqk', qb, k).astype(jnp.float32) * scale + decay
        sc = jnp.where(dist >= 0, sc, -jnp.inf)
        p = jax.nn.softmax(sc, axis=-1)
        return jnp.einsum('bhqk,bkhd->bqhd', p.astype(v.dtype), v)

    out = lax.map(one_block, (_to_blocks(q), _to_blocks(c), jnp.arange(s_len // Q_BLOCK)))
    return _from_blocks(out)


def setup_inputs(seed: int = 0) -> dict:
    key = jax.random.key(seed)
    ks = jax.random.split(key, 32)
    L = DEPTH

    def nrm(k, shape, scale):
        return jax.random.normal(k, shape, jnp.float32) * scale

    a_target = jax.random.uniform(ks[9], (L, D_RNN), jnp.float32, 0.9, 0.999)
    sig = a_target ** (1.0 / RG_LRU_C)
    return {
        "x": nrm(ks[0], (BATCH, SEQ, D_MODEL), 1.0),
        "norm1_g": 1.0 + nrm(ks[1], (L, D_MODEL), 0.02),
        "w_in": nrm(ks[2], (L, D_MODEL, N_IN), D_MODEL ** -0.5),
        "rnn_conv_w": nrm(ks[3], (L, CONV_RNN, D_RNN), CONV_RNN ** -0.5),
        "rnn_conv_b": nrm(ks[4], (L, D_RNN), 0.02),
        "rg_w_r": nrm(ks[5], (L, RNN_BLOCKS, RNN_BLOCK_W, RNN_BLOCK_W), RNN_BLOCK_W ** -0.5),
        "rg_b_r": nrm(ks[6], (L, D_RNN), 0.02),
        "rg_w_i": nrm(ks[7], (L, RNN_BLOCKS, RNN_BLOCK_W, RNN_BLOCK_W), RNN_BLOCK_W ** -0.5),
        "rg_b_i": nrm(ks[8], (L, D_RNN), 0.02),
        "rg_a": jnp.log(sig) - jnp.log1p(-sig),
        "diff_lq1": nrm(ks[10], (L, DIFF_HEAD_DIM), 0.1),
        "diff_lk1": nrm(ks[11], (L, DIFF_HEAD_DIM), 0.1),
        "diff_lq2": nrm(ks[12], (L, DIFF_HEAD_DIM), 0.1),
        "diff_lk2": nrm(ks[13], (L, DIFF_HEAD_DIM), 0.1),
        "diff_subln_g": 1.0 + nrm(ks[14], (L, 2 * DIFF_HEAD_DIM), 0.02),
        "rel_bias": nrm(ks[15], (N_BUCKETS, DIFF_HEADS), 0.5),
        "fox_b_f": 3.0 + nrm(ks[16], (L, FOX_HEADS), 0.5),
        "gate_b": nrm(ks[17], (L, N_BRANCH, D_MODEL), 0.02),
        "w_br_rnn": nrm(ks[18], (L, D_RNN, D_MODEL), D_RNN ** -0.5),
        "w_br_diff": nrm(ks[19], (L, DIFF_WIDTH, D_MODEL), DIFF_WIDTH ** -0.5),
        "w_br_fox": nrm(ks[20], (L, FOX_WIDTH, D_MODEL), FOX_WIDTH ** -0.5),
        "w_out": nrm(ks[21], (L, D_MODEL, D_MODEL), D_MODEL ** -0.5),
        "norm2_g": 1.0 + nrm(ks[22], (L, D_MODEL), 0.02),
        "ffn_up": nrm(ks[23], (L, D_MODEL, 2 * D_FF), D_MODEL ** -0.5),
        "ffn_conv_w": nrm(ks[24], (L, CONV_FFN, 2 * D_FF), CONV_FFN ** -0.5),
        "ffn_conv_b": nrm(ks[25], (L, 2 * D_FF), 0.02),
        "ffn_down": nrm(ks[26], (L, D_FF, D_MODEL), D_FF ** -0.5),
        "final_g": 1.0 + nrm(ks[27], (D_MODEL,), 0.02),
    }


def reference(x, norm1_g, w_in, rnn_conv_w, rnn_conv_b, rg_w_r, rg_b_r, rg_w_i, rg_b_i, rg_a,
              diff_lq1, diff_lk1, diff_lq2, diff_lk2, diff_subln_g, rel_bias, fox_b_f, gate_b,
              w_br_rnn, w_br_diff, w_br_fox, w_out, norm2_g, ffn_up, ffn_conv_w, ffn_conv_b,
              ffn_down, final_g):
    bsz, s_len, _ = x.shape
    for l in range(DEPTH):
        h = rmsnorm(x, norm1_g[l])
        proj = h @ w_in[l]
        (x_rnn, g_rnn, dq, dk, dv, fq, fk, fv, f_logit, gates) = jnp.split(proj, IN_SPLITS, axis=-1)

        x_rnn = causal_dwconv(x_rnn, rnn_conv_w[l], rnn_conv_b[l])
        y_rnn = jax.nn.gelu(g_rnn) * rg_lru(x_rnn, rg_w_r[l], rg_b_r[l], rg_w_i[l], rg_b_i[l], rg_a[l])

        lam_init = 0.8 - 0.6 * math.exp(-0.3 * l)
        lam = (jnp.exp(jnp.sum(diff_lq1[l] * diff_lk1[l]).astype(jnp.float32))
               - jnp.exp(jnp.sum(diff_lq2[l] * diff_lk2[l]).astype(jnp.float32)) + lam_init)
        o_diff = diff_attention(dq.reshape(bsz, s_len, DIFF_HEADS, 2, DIFF_HEAD_DIM),
                                dk.reshape(bsz, s_len, DIFF_HEADS, 2, DIFF_HEAD_DIM),
                                dv.reshape(bsz, s_len, DIFF_HEADS, 2 * DIFF_HEAD_DIM), lam, rel_bias)
        y_diff = (rmsnorm(o_diff, diff_subln_g[l]) * (1.0 - lam_init)).reshape(bsz, s_len, DIFF_WIDTH)

        log_f = jax.nn.log_sigmoid((f_logit + fox_b_f[l]).astype(jnp.float32))
        o_fox = forgetting_attention(fq.reshape(bsz, s_len, FOX_HEADS, FOX_HEAD_DIM),
                                     fk.reshape(bsz, s_len, FOX_HEADS, FOX_HEAD_DIM),
                                     fv.reshape(bsz, s_len, FOX_HEADS, FOX_HEAD_DIM), log_f)
        y_fox = o_fox.reshape(bsz, s_len, FOX_WIDTH)

        g = jax.nn.sigmoid(gates.reshape(bsz, s_len, N_BRANCH, D_MODEL) + gate_b[l])
        m = (g[:, :, 0] * (y_rnn @ w_br_rnn[l])
             + g[:, :, 1] * (y_diff @ w_br_diff[l])
             + g[:, :, 2] * (y_fox @ w_br_fox[l]))
        x = x + m @ w_out[l]

        h = rmsnorm(x, norm2_g[l])
        u = causal_dwconv(h @ ffn_up[l], ffn_conv_w[l], ffn_conv_b[l])
        u_gate, u_val = jnp.split(u, 2, axis=-1)
        x = x + (jax.nn.gelu(u_gate) * u_val) @ ffn_down[l]
    return rmsnorm(x, final_g)
```

```python
import functools
import math

import jax
import jax.numpy as jnp
from jax import lax
from jax.experimental import pallas as pl
from jax.experimental.pallas import tpu as pltpu

F32 = jnp.float32
BF16 = jnp.bfloat16

D_MODEL = 1024
DEPTH = 2
D_RNN = D_MODEL
RNN_BLOCKS = 8
RNN_BLOCK_W = D_RNN // RNN_BLOCKS
RG_LRU_C = 8.0
DIFF_HEADS = 4
DIFF_HEAD_DIM = 64
DIFF_QK = DIFF_HEADS * 2 * DIFF_HEAD_DIM
DIFF_WIDTH = DIFF_HEADS * 2 * DIFF_HEAD_DIM
FOX_HEADS = 8
FOX_HEAD_DIM = 64
FOX_WIDTH = FOX_HEADS * FOX_HEAD_DIM
N_BUCKETS = 32
MAX_EXACT = N_BUCKETS // 2
MAX_DISTANCE = 128
D_FF = ((8 * D_MODEL // 3 + 127) // 128) * 128
N_BRANCH = 3
EPS = 1e-6
LOG2E = 1.4426950408889634
NEG = -1e30

LANES = 128
SUBLANES = 8

COL_XRNN = 0
COL_GRNN = D_RNN
COL_DQ = 2 * D_RNN
COL_DK = COL_DQ + DIFF_QK
COL_DV = COL_DK + DIFF_QK
COL_FQ = COL_DV + DIFF_WIDTH
COL_FK = COL_FQ + FOX_WIDTH
COL_FV = COL_FK + FOX_WIDTH
COL_FLOG = COL_FV + FOX_WIDTH
COL_GATES = COL_FLOG
N_MAIN = COL_GATES + N_BRANCH * D_MODEL

ATT_TILE = 256
TM_PROJ = 1024
TN_PROJ = 1024
TM_MERGE = 512
TM_FFN = 512
FF_CHUNK = D_FF // 2
RNN_HALF = D_RNN // 2
VMEM_LIMIT = 56 * 1024 * 1024


def _softplus(z):
    return jnp.maximum(z, 0.0) + jnp.log1p(jnp.exp(-jnp.abs(z)))


def _rms(x, g):
    return x * lax.rsqrt(jnp.mean(x * x, axis=-1, keepdims=True) + EPS) * g


def _inproj_body(x_ref, g_ref, w_ref, wf_ref, out_ref, flog_ref, h_scr):
    @pl.when(pl.program_id(1) == 0)
    def _():
        hb = _rms(x_ref[...], g_ref[...]).astype(BF16)
        h_scr[...] = hb
        flog_ref[...] = jnp.dot(hb, wf_ref[...], preferred_element_type=F32)

    out_ref[...] = jnp.dot(h_scr[...], w_ref[...], preferred_element_type=F32).astype(out_ref.dtype)


def _inproj(x2, g, w_main, w_f):
    t, d = x2.shape
    n = w_main.shape[1]
    tm = min(TM_PROJ, t)
    return pl.pallas_call(
        _inproj_body,
        grid=(t // tm, n // TN_PROJ),
        in_specs=[
            pl.BlockSpec((tm, d), lambda i, j: (i, 0)),
            pl.BlockSpec((1, d), lambda i, j: (0, 0)),
            pl.BlockSpec((d, TN_PROJ), lambda i, j: (0, j)),
            pl.BlockSpec((d, LANES), lambda i, j: (0, 0)),
        ],
        out_specs=[
            pl.BlockSpec((tm, TN_PROJ), lambda i, j: (i, j)),
            pl.BlockSpec((tm, LANES), lambda i, j: (i, 0)),
        ],
        out_shape=[
            jax.ShapeDtypeStruct((t, n), BF16),
            jax.ShapeDtypeStruct((t, LANES), F32),
        ],
        scratch_shapes=[pltpu.VMEM((tm, d), BF16)],
        compiler_params=pltpu.CompilerParams(
            dimension_semantics=("arbitrary", "arbitrary"), vmem_limit_bytes=VMEM_LIMIT),
        name="inproj",
    )(x2, g, w_main, w_f)


def _rnn_body(xr_ref, gr_ref, cw_ref, cb_ref, wr_ref, br_ref, wi_ref, bi_ref, ap_ref, y_ref, a_scr, u_scr):
    s, width = xr_ref.shape
    nblk = width // RNN_BLOCK_W
    half = pl.program_id(1)
    row = lax.broadcasted_iota(jnp.int32, (s, RNN_BLOCK_W), 0)
    for n in range(nblk):
        sl = slice(n * RNN_BLOCK_W, (n + 1) * RNN_BLOCK_W)
        x = xr_ref[:, sl].astype(F32)
        x1 = jnp.where(row >= 1, pltpu.roll(x, 1, 0), 0.0)
        x2 = jnp.where(row >= 2, pltpu.roll(x, 2, 0), 0.0)
        x3 = jnp.where(row >= 3, pltpu.roll(x, 3, 0), 0.0)
        xc = (cb_ref[:, sl] + x3 * cw_ref[0:1, sl] + x2 * cw_ref[1:2, sl]
              + x1 * cw_ref[2:3, sl] + x * cw_ref[3:4, sl])
        xb = xc.astype(BF16)
        blk = half * nblk + n
        r = jax.nn.sigmoid(jnp.dot(xb, wr_ref[blk], preferred_element_type=F32) + br_ref[:, sl])
        gi = jax.nn.sigmoid(jnp.dot(xb, wi_ref[blk], preferred_element_type=F32) + bi_ref[:, sl])
        log_a = (-RG_LRU_C * r) * _softplus(-ap_ref[:, sl])
        a = jnp.exp(log_a)
        a_scr[:, sl] = a
        u_scr[:, sl] = jnp.sqrt(1.0 - a * a) * (gi * xc)

    row8 = lax.broadcasted_iota(jnp.int32, (SUBLANES, width), 0)

    def group(k, hprev):
        off = pl.multiple_of(k * SUBLANES, SUBLANES)
        a = a_scr[pl.ds(off, SUBLANES), :]
        u = u_scr[pl.ds(off, SUBLANES), :]
        for d in (1, 2, 4):
            a_sh = jnp.where(row8 >= d, pltpu.roll(a, d, 0), 1.0)
            u_sh = jnp.where(row8 >= d, pltpu.roll(u, d, 0), 0.0)
            u = a * u_sh + u
            a = a * a_sh
        h = a * hprev + u
        u_scr[pl.ds(off, SUBLANES), :] = h
        return h[SUBLANES - 1:SUBLANES, :]

    lax.fori_loop(0, s // SUBLANES, group, jnp.zeros((1, width), F32))

    for n in range(nblk):
        sl = slice(n * RNN_BLOCK_W, (n + 1) * RNN_BLOCK_W)
        y_ref[:, sl] = (jax.nn.gelu(gr_ref[:, sl].astype(F32)) * u_scr[:, sl]).astype(y_ref.dtype)


def _rnn(main3, conv_w, conv_b, w_r, b_r, w_i, b_i, a_param):
    b, s, _ = main3.shape
    w = RNN_HALF
    nh = D_RNN // w
    vec = lambda k: pl.BlockSpec((k, w), lambda i, j: (0, j))
    full3 = pl.BlockSpec((RNN_BLOCKS, RNN_BLOCK_W, RNN_BLOCK_W), lambda i, j: (0, 0, 0))
    return pl.pallas_call(
        _rnn_body,
        grid=(b, nh),
        in_specs=[
            pl.BlockSpec((None, s, w), lambda i, j: (i, 0, COL_XRNN // w + j)),
            pl.BlockSpec((None, s, w), lambda i, j: (i, 0, COL_GRNN // w + j)),
            vec(4), vec(1), full3, vec(1), full3, vec(1), vec(1),
        ],
        out_specs=pl.BlockSpec((None, s, w), lambda i, j: (i, 0, j)),
        out_shape=jax.ShapeDtypeStruct((b, s, D_RNN), BF16),
        scratch_shapes=[pltpu.VMEM((s, w), F32), pltpu.VMEM((s, w), F32)],
        compiler_params=pltpu.CompilerParams(
            dimension_semantics=("arbitrary", "arbitrary"), vmem_limit_bytes=VMEM_LIMIT),
        name="rnn",
    )(main3, main3, conv_w, conv_b, w_r, b_r, w_i, b_i, a_param)


def _cum_body(flog_ref, fb_ref, c_ref, ct_ref):
    s = flog_ref.shape[0]
    z = flog_ref[...] + fb_ref[...]
    c = -_softplus(-z) * LOG2E
    row = lax.broadcasted_iota(jnp.int32, c.shape, 0)
    d = 1
    while d < s:
        c = c + jnp.where(row >= d, pltpu.roll(c, d, 0), 0.0)
        d *= 2
    c_ref[...] = c
    ct_ref[...] = c.T[:SUBLANES, :]


def _cum(flog3, fb):
    b, s, _ = flog3.shape
    return pl.pallas_call(
        _cum_body,
        grid=(b,),
        in_specs=[
            pl.BlockSpec((None, s, LANES), lambda i: (i, 0, 0)),
            pl.BlockSpec((1, LANES), lambda i: (0, 0)),
        ],
        out_specs=[
            pl.BlockSpec((None, s, LANES), lambda i: (i, 0, 0)),
            pl.BlockSpec((None, SUBLANES, s), lambda i: (i, 0, 0)),
        ],
        out_shape=[
            jax.ShapeDtypeStruct((b, s, LANES), F32),
            jax.ShapeDtypeStruct((b, SUBLANES, s), F32),
        ],
        compiler_params=pltpu.CompilerParams(dimension_semantics=("arbitrary",)),
        name="fox_cumsum",
    )(flog3, fb)


def _flash_two_maps(q_ref, k_ref, v_ref, q0, i, t, bias_fn):
    lane = lax.broadcasted_iota(jnp.int32, (t, LANES), 1)
    q = q_ref[pl.ds(q0, t), :]
    zero = jnp.zeros_like(q)
    qa = jnp.where(lane < DIFF_HEAD_DIM, q, zero)
    qb = jnp.where(lane < DIFF_HEAD_DIM, zero, q)

    def update(qx, kt, vt, bias, m, l, acc):
        sc = lax.dot_general(qx, kt, (((1,), (1,)), ((), ())), preferred_element_type=F32) + bias
        m_new = jnp.maximum(m, jnp.max(sc, axis=-1, keepdims=True))
        alpha = jnp.exp2(m - m_new)
        p = jnp.exp2(sc - m_new)
        l = alpha * l + jnp.sum(p, axis=-1, keepdims=True)
        acc = alpha * acc + jnp.dot(p.astype(BF16), vt, preferred_element_type=F32)
        return m_new, l, acc

    def kv_step(j, st):
        ma, la, acca, mb, lb, accb = st
        k0 = pl.multiple_of(j * t, t)
        kt = k_ref[pl.ds(k0, t), :]
        vt = v_ref[pl.ds(k0, t), :]
        ma, la, acca = update(qa, kt, vt, bias_fn(j, 0), ma, la, acca)
        mb, lb, accb = update(qb, kt, vt, bias_fn(j, 1), mb, lb, accb)
        return ma, la, acca, mb, lb, accb

    m0 = jnp.full((t, 1), NEG, F32)
    l0 = jnp.zeros((t, 1), F32)
    a0 = jnp.zeros((t, LANES), F32)
    _, la, acca, _, lb, accb = lax.fori_loop(0, i + 1, kv_step, (m0, l0, a0, m0, l0, a0))
    return acca, la, accb, lb


def _diff_body(rel_ref, q_ref, k_ref, v_ref, lq1_ref, lk1_ref, lq2_ref, lk2_ref, sg_ref, o_ref, bias_scr,
               *, lam_init):
    s = q_ref.shape[0]
    t = ATT_TILE
    head = pl.program_id(0)

    @pl.when(pl.program_id(1) == 0)
    def _():
        qi = lax.broadcasted_iota(jnp.int32, (t, t), 0)
        ki = lax.broadcasted_iota(jnp.int32, (t, t), 1)
        far = rel_ref[N_BUCKETS - 1, head]
        bias_scr[0] = jnp.zeros((t, t), F32)
        for idx, off in ((1, t), (2, 0)):
            dist = qi - ki + off
            n = jnp.maximum(dist, 0)
            nf = jnp.maximum(n, 1).astype(F32)
            large = MAX_EXACT + (jnp.log(nf / MAX_EXACT) / math.log(MAX_DISTANCE / MAX_EXACT)
                                 * (N_BUCKETS - MAX_EXACT)).astype(jnp.int32)
            large = jnp.minimum(large, N_BUCKETS - 1)
            bucket = jnp.where(n < MAX_EXACT, n, large)
            bias = jnp.zeros((t, t), F32)
            for bk in range(N_BUCKETS):
                bias = jnp.where(bucket == bk, rel_ref[bk, head], bias)
            bias = (bias - far) * LOG2E
            bias_scr[idx] = jnp.where(dist >= 0, bias, NEG)

    lam = (jnp.exp(jnp.sum(lq1_ref[...] * lk1_ref[...], axis=-1, keepdims=True))
           - jnp.exp(jnp.sum(lq2_ref[...] * lk2_ref[...], axis=-1, keepdims=True)) + lam_init)

    def q_tile(i, carry):
        q0 = pl.multiple_of(i * t, t)
        bias_fn = lambda j, which: bias_scr[jnp.maximum(j - i + 2, 0)]
        acca, la, accb, lb = _flash_two_maps(q_ref, k_ref, v_ref, q0, i, t, bias_fn)
        o = acca / la - lam * (accb / lb)
        o_ref[pl.ds(q0, t), :] = (_rms(o, sg_ref[...]) * (1.0 - lam_init)).astype(o_ref.dtype)
        return carry

    lax.fori_loop(0, s // t, q_tile, 0)


def _diff_attn(main3, rel_bias, lq1, lk1, lq2, lk2, subln_g, lam_init):
    b, s, _ = main3.shape
    col = lambda base: pl.BlockSpec((None, s, LANES), lambda h, i: (i, 0, base // LANES + h))
    small = lambda w: pl.BlockSpec((1, w), lambda h, i: (0, 0))
    return pl.pallas_call(
        functools.partial(_diff_body, lam_init=lam_init),
        grid=(DIFF_HEADS, b),
        in_specs=[
            pl.BlockSpec(memory_space=pltpu.SMEM),
            col(COL_DQ), col(COL_DK), col(COL_DV),
            small(DIFF_HEAD_DIM), small(DIFF_HEAD_DIM), small(DIFF_HEAD_DIM), small(DIFF_HEAD_DIM),
            small(2 * DIFF_HEAD_DIM),
        ],
        out_specs=pl.BlockSpec((None, s, LANES), lambda h, i: (i, 0, h)),
        out_shape=jax.ShapeDtypeStruct((b, s, DIFF_WIDTH), BF16),
        scratch_shapes=[pltpu.VMEM((3, ATT_TILE, ATT_TILE), F32)],
        compiler_params=pltpu.CompilerParams(dimension_semantics=("arbitrary", "arbitrary")),
        name="diff_attn",
    )(rel_bias, main3, main3, main3, lq1, lk1, lq2, lk2, subln_g)


def _fox_body(q_ref, k_ref, v_ref, c_ref, ct_ref, o_ref, mask_scr):
    s = q_ref.shape[0]
    t = ATT_TILE
    pair = pl.program_id(0)

    @pl.when(pl.program_id(1) == 0)
    def _():
        qi = lax.broadcasted_iota(jnp.int32, (t, t), 0)
        ki = lax.broadcasted_iota(jnp.int32, (t, t), 1)
        mask_scr[0] = jnp.zeros((t, t), F32)
        mask_scr[1] = jnp.where(qi >= ki, 0.0, NEG)

    lane = lax.broadcasted_iota(jnp.int32, (t, LANES), 1)
    sub = lax.broadcasted_iota(jnp.int32, (SUBLANES, t), 0)

    def q_tile(i, carry):
        q0 = pl.multiple_of(i * t, t)
        c_tile = c_ref[pl.ds(q0, t), :]
        cq = [jnp.sum(jnp.where(lane == 2 * pair + w, c_tile, 0.0), axis=-1, keepdims=True) for w in (0, 1)]

        def bias_fn(j, which):
            k0 = pl.multiple_of(j * t, t)
            ct_tile = ct_ref[:, pl.ds(k0, t)]
            ck = jnp.sum(jnp.where(sub == 2 * pair + which, ct_tile, 0.0), axis=0, keepdims=True)
            return mask_scr[jnp.maximum(j - i + 1, 0)] + (cq[which] - ck)

        acca, la, accb, lb = _flash_two_maps(q_ref, k_ref, v_ref, q0, i, t, bias_fn)
        o = jnp.where(lane < FOX_HEAD_DIM, acca / la, accb / lb)
        o_ref[pl.ds(q0, t), :] = o.astype(o_ref.dtype)
        return carry

    lax.fori_loop(0, s // t, q_tile, 0)


def _fox_attn(main3, c, ct):
    b, s, _ = main3.shape
    col = lambda base: pl.BlockSpec((None, s, LANES), lambda p, i: (i, 0, base // LANES + p))
    return pl.pallas_call(
        _fox_body,
        grid=(FOX_HEADS // 2, b),
        in_specs=[
            col(COL_FQ), col(COL_FK), col(COL_FV),
            pl.BlockSpec((None, s, LANES), lambda p, i: (i, 0, 0)),
            pl.BlockSpec((None, SUBLANES, s), lambda p, i: (i, 0, 0)),
        ],
        out_specs=pl.BlockSpec((None, s, LANES), lambda p, i: (i, 0, p)),
        out_shape=jax.ShapeDtypeStruct((b, s, FOX_WIDTH), BF16),
        scratch_shapes=[pltpu.VMEM((2, ATT_TILE, ATT_TILE), F32)],
        compiler_params=pltpu.CompilerParams(dimension_semantics=("arbitrary", "arbitrary")),
        name="fox_attn",
    )(main3, main3, main3, c, ct)


def _merge_body(x_ref, g0_ref, g1_ref, g2_ref, gb_ref, yr_ref, yd_ref, yf_ref, wr_ref, wd_ref, wf_ref, wo_ref,
                o_ref):
    def branch(g_ref, k, y_ref, w_ref):
        gate = jax.nn.sigmoid(g_ref[...].astype(F32) + gb_ref[k:k + 1, :])
        return gate * jnp.dot(y_ref[...], w_ref[...], preferred_element_type=F32)

    m = branch(g0_ref, 0, yr_ref, wr_ref) + branch(g1_ref, 1, yd_ref, wd_ref) + branch(g2_ref, 2, yf_ref, wf_ref)
    o_ref[...] = x_ref[...] + jnp.dot(m.astype(BF16), wo_ref[...], preferred_element_type=F32)


def _merge(x2, main2, gate_b, y_rnn, y_diff, y_fox, w_r, w_d, w_f, w_o):
    t, d = x2.shape
    tm = min(TM_MERGE, t)
    rows = lambda w, c=0: pl.BlockSpec((tm, w), lambda i: (i, c))
    whole = lambda a: pl.BlockSpec(a.shape, lambda i: (0, 0))
    g0 = COL_GATES // d
    return pl.pallas_call(
        _merge_body,
        grid=(t // tm,),
        in_specs=[
            rows(d), rows(d, g0), rows(d, g0 + 1), rows(d, g0 + 2), whole(gate_b),
            rows(D_RNN), rows(DIFF_WIDTH), rows(FOX_WIDTH),
            whole(w_r), whole(w_d), whole(w_f), whole(w_o),
        ],
        out_specs=rows(d),
        out_shape=jax.ShapeDtypeStruct((t, d), F32),
        compiler_params=pltpu.CompilerParams(dimension_semantics=("arbitrary",), vmem_limit_bytes=VMEM_LIMIT),
        name="merge",
    )(x2, main2, main2, main2, gate_b, y_rnn, y_diff, y_fox, w_r, w_d, w_f, w_o)


def _ffn_body(x_ref, g_ref, wg_ref, wv_ref, cwg_ref, cwv_ref, cbg_ref, cbv_ref, wd_ref, o_ref,
              h_scr, acc_scr, carry_g, carry_v, *, tiles_per_seq):
    i = pl.program_id(0)
    c = pl.program_id(1)
    nc = pl.num_programs(1)
    tm = x_ref.shape[0]
    cw = wg_ref.shape[1]

    @pl.when(c == 0)
    def _():
        h_scr[...] = _rms(x_ref[...], g_ref[...]).astype(BF16)

    row = lax.broadcasted_iota(jnp.int32, (tm, cw), 0)

    @pl.when((i % tiles_per_seq) == 0)
    def _():
        carry_g[c] = jnp.zeros((SUBLANES, cw), F32)
        carry_v[c] = jnp.zeros((SUBLANES, cw), F32)

    def conv(w_ref, cw_ref, cb_ref, carry_ref):
        u = jnp.dot(h_scr[...], w_ref[...], preferred_element_type=F32)
        prev = carry_ref[c]
        carry_ref[c] = u[tm - SUBLANES:, :]
        u1 = jnp.where(row == 0, prev[SUBLANES - 1:, :], pltpu.roll(u, 1, 0))
        u2 = jnp.where(row == 0, prev[SUBLANES - 2:SUBLANES - 1, :],
                       jnp.where(row == 1, prev[SUBLANES - 1:, :], pltpu.roll(u, 2, 0)))
        return cb_ref[...] + u2 * cw_ref[0:1, :] + u1 * cw_ref[1:2, :] + u * cw_ref[2:3, :]

    ug = conv(wg_ref, cwg_ref, cbg_ref, carry_g)
    uv = conv(wv_ref, cwv_ref, cbv_ref, carry_v)
    act = (jax.nn.gelu(ug) * uv).astype(BF16)
    part = jnp.dot(act, wd_ref[...], preferred_element_type=F32)

    @pl.when(c == 0)
    def _():
        acc_scr[...] = x_ref[...] + part

    @pl.when(c > 0)
    def _():
        acc_scr[...] += part

    @pl.when(c == nc - 1)
    def _():
        o_ref[...] = acc_scr[...]


def _ffn(x2, g, w_up, conv_w, conv_b, w_down, seq_len):
    t, d = x2.shape
    tm = min(TM_FFN, seq_len)
    cw = FF_CHUNK
    nc = D_FF // cw
    return pl.pallas_call(
        functools.partial(_ffn_body, tiles_per_seq=seq_len // tm),
        grid=(t // tm, nc),
        in_specs=[
            pl.BlockSpec((tm, d), lambda i, c: (i, 0)),
            pl.BlockSpec((1, d), lambda i, c: (0, 0)),
            pl.BlockSpec((d, cw), lambda i, c: (0, c)),
            pl.BlockSpec((d, cw), lambda i, c: (0, nc + c)),
            pl.BlockSpec((3, cw), lambda i, c: (0, c)),
            pl.BlockSpec((3, cw), lambda i, c: (0, nc + c)),
            pl.BlockSpec((1, cw), lambda i, c: (0, c)),
            pl.BlockSpec((1, cw), lambda i, c: (0, nc + c)),
            pl.BlockSpec((cw, d), lambda i, c: (c, 0)),
        ],
        out_specs=pl.BlockSpec((tm, d), lambda i, c: (i, 0)),
        out_shape=jax.ShapeDtypeStruct((t, d), F32),
        scratch_shapes=[
            pltpu.VMEM((tm, d), BF16),
            pltpu.VMEM((tm, d), F32),
            pltpu.VMEM((nc, SUBLANES, cw), F32),
            pltpu.VMEM((nc, SUBLANES, cw), F32),
        ],
        compiler_params=pltpu.CompilerParams(
            dimension_semantics=("arbitrary", "arbitrary"), vmem_limit_bytes=VMEM_LIMIT),
        name="ffn",
    )(x2, g, w_up, w_up, conv_w, conv_w, conv_b, conv_b, w_down)


def _final_norm_body(x_ref, g_ref, o_ref):
    o_ref[...] = _rms(x_ref[...], g_ref[...])


def _final_norm(x2, g):
    t, d = x2.shape
    tm = min(TM_PROJ, t)
    return pl.pallas_call(
        _final_norm_body,
        grid=(t // tm,),
        in_specs=[pl.BlockSpec((tm, d), lambda i: (i, 0)), pl.BlockSpec((1, d), lambda i: (0, 0))],
        out_specs=pl.BlockSpec((tm, d), lambda i: (i, 0)),
        out_shape=jax.ShapeDtypeStruct((t, d), F32),
        compiler_params=pltpu.CompilerParams(dimension_semantics=("arbitrary",)),
        name="final_norm",
    )(x2, g)


def _prep_w_in(w_in_l):
    colscale = jnp.ones((N_MAIN,), F32)
    colscale = colscale.at[COL_DQ:COL_DK].set(DIFF_HEAD_DIM ** -0.5 * LOG2E)
    colscale = colscale.at[COL_FQ:COL_FK].set(FOX_HEAD_DIM ** -0.5 * LOG2E)
    w_main = jnp.concatenate([w_in_l[:, :COL_FLOG], w_in_l[:, COL_FLOG + FOX_HEADS:]], axis=1) * colscale
    w_f = jnp.pad(w_in_l[:, COL_FLOG:COL_FLOG + FOX_HEADS], ((0, 0), (0, LANES - FOX_HEADS)))
    return w_main.astype(BF16), w_f.astype(BF16)


def kernel(x, norm1_g, w_in, rnn_conv_w, rnn_conv_b, rg_w_r, rg_b_r, rg_w_i, rg_b_i, rg_a,
           diff_lq1, diff_lk1, diff_lq2, diff_lk2, diff_subln_g, rel_bias, fox_b_f, gate_b,
           w_br_rnn, w_br_diff, w_br_fox, w_out, norm2_g, ffn_up, ffn_conv_w, ffn_conv_b,
           ffn_down, final_g):
    bsz, s_len, d = x.shape
    t = bsz * s_len
    x2 = x.reshape(t, d)
    row = lambda v: v.reshape(1, -1)
    for l in range(DEPTH):
        w_main, w_f = _prep_w_in(w_in[l])
        main2, flog = _inproj(x2, row(norm1_g[l]), w_main, w_f)
        main3 = main2.reshape(bsz, s_len, N_MAIN)

        y_rnn = _rnn(main3, rnn_conv_w[l], row(rnn_conv_b[l]), rg_w_r[l].astype(BF16), row(rg_b_r[l]),
                     rg_w_i[l].astype(BF16), row(rg_b_i[l]), row(rg_a[l]))

        lam_init = 0.8 - 0.6 * math.exp(-0.3 * l)
        y_diff = _diff_attn(main3, rel_bias, row(diff_lq1[l]), row(diff_lk1[l]), row(diff_lq2[l]),
                            row(diff_lk2[l]), row(diff_subln_g[l]), lam_init)

        fb = jnp.pad(fox_b_f[l], (0, LANES - FOX_HEADS)).reshape(1, LANES)
        c, ct = _cum(flog.reshape(bsz, s_len, LANES), fb)
        y_fox = _fox_attn(main3, c, ct)

        x2 = _merge(x2, main2, gate_b[l], y_rnn.reshape(t, D_RNN), y_diff.reshape(t, DIFF_WIDTH),
                    y_fox.reshape(t, FOX_WIDTH), w_br_rnn[l].astype(BF16), w_br_diff[l].astype(BF16),
                    w_br_fox[l].astype(BF16), w_out[l].astype(BF16))

        x2 = _ffn(x2, row(norm2_g[l]), ffn_up[l].astype(BF16), ffn_conv_w[l], row(ffn_conv_b[l]),
                  ffn_down[l].astype(BF16), s_len)
    return _final_norm(x2, row(final_g)).reshape(bsz, s_len, d)
```

```python
import functools
import math

import jax
import jax.numpy as jnp
from jax import lax
from jax.experimental import pallas as pl
from jax.experimental.pallas import tpu as pltpu

F32 = jnp.float32
BF16 = jnp.bfloat16

D_MODEL = 1024
DEPTH = 2
D_RNN = D_MODEL
RNN_BLOCKS = 8
RNN_BLOCK_W = D_RNN // RNN_BLOCKS
RG_LRU_C = 8.0
DIFF_HEADS = 4
DIFF_HEAD_DIM = 64
DIFF_QK = DIFF_HEADS * 2 * DIFF_HEAD_DIM
DIFF_WIDTH = DIFF_HEADS * 2 * DIFF_HEAD_DIM
FOX_HEADS = 8
FOX_HEAD_DIM = 64
FOX_WIDTH = FOX_HEADS * FOX_HEAD_DIM
N_BUCKETS = 32
MAX_EXACT = N_BUCKETS // 2
MAX_DISTANCE = 128
D_FF = ((8 * D_MODEL // 3 + 127) // 128) * 128
N_BRANCH = 3
EPS = 1e-6
LOG2E = 1.4426950408889634
NEG = -1e30

LANES = 128
SUBLANES = 8
BF16_ROWS = 16

W_XRNN = 0
W_GRNN = W_XRNN + D_RNN
W_DQ = W_GRNN + D_RNN
W_DK = W_DQ + DIFF_QK
W_DV = W_DK + DIFF_QK
W_FQ = W_DV + DIFF_WIDTH
W_FK = W_FQ + FOX_WIDTH
W_FV = W_FK + FOX_WIDTH
W_FLOG = W_FV + FOX_WIDTH
W_GATES = W_FLOG + FOX_HEADS

COL_XRNN = 0
COL_GRNN = COL_XRNN + D_RNN
COL_DK = COL_GRNN + D_RNN
COL_FK = COL_DK + DIFF_QK
COL_GATES = COL_FK + FOX_WIDTH
N_ROWOUT = COL_GATES + N_BRANCH * D_MODEL
ROW_DQ = 0
ROW_FQ = ROW_DQ + DIFF_QK
ROW_DV = ROW_FQ + FOX_WIDTH
ROW_FV = ROW_DV + DIFF_WIDTH
N_TOUT = ROW_FV + FOX_WIDTH

TQ = 512
TK = 256
TM_PROJ = 1024
TN_PROJ = 1024
TM_MERGE = 512
TM_FFN = 512
FF_CHUNK = D_FF // 2
RNN_HALF = D_RNN // 2
VMEM_LIMIT = 56 * 1024 * 1024

AUG_GROUP = 8
AUG_SLOTS = 6
AUG_PARTS = 3


def _softplus(z):
    return jnp.maximum(z, 0.0) + jnp.log1p(jnp.exp(-jnp.abs(z)))


def _rms(x, g):
    return x * lax.rsqrt(jnp.mean(x * x, axis=-1, keepdims=True) + EPS) * g


def _inproj_body(x_ref, g_ref, w_ref, wt_ref, wf_ref, out_ref, outt_ref, flog_ref, h_scr, *, n_row):
    j = pl.program_id(1)

    @pl.when(j == 0)
    def _():
        hb = _rms(x_ref[...], g_ref[...]).astype(BF16)
        h_scr[...] = hb
        flog_ref[...] = jnp.dot(hb, wf_ref[...], preferred_element_type=F32)

    @pl.when(j < n_row)
    def _():
        out_ref[...] = jnp.dot(h_scr[...], w_ref[...], preferred_element_type=F32).astype(out_ref.dtype)

    @pl.when(j >= n_row)
    def _():
        outt_ref[...] = lax.dot_general(wt_ref[...], h_scr[...], (((1,), (1,)), ((), ())),
                                        preferred_element_type=F32).astype(outt_ref.dtype)


def _inproj(x2, g, w_row, w_t, w_f, bsz, s_len):
    t, d = x2.shape
    tm = min(TM_PROJ, s_len)
    tps = s_len // tm
    n_row = N_ROWOUT // TN_PROJ
    n_t = N_TOUT // TN_PROJ
    return pl.pallas_call(
        functools.partial(_inproj_body, n_row=n_row),
        grid=(t // tm, n_row + n_t),
        in_specs=[
            pl.BlockSpec((tm, d), lambda i, j: (i, 0)),
            pl.BlockSpec((1, d), lambda i, j: (0, 0)),
            pl.BlockSpec((d, TN_PROJ), lambda i, j: (0, jnp.minimum(j, n_row - 1))),
            pl.BlockSpec((TN_PROJ, d), lambda i, j: (jnp.maximum(j - n_row, 0), 0)),
            pl.BlockSpec((d, LANES), lambda i, j: (0, 0)),
        ],
        out_specs=[
            pl.BlockSpec((tm, TN_PROJ), lambda i, j: (i, jnp.minimum(j, n_row - 1))),
            pl.BlockSpec((None, TN_PROJ, tm), lambda i, j: (i // tps, jnp.maximum(j - n_row, 0), i % tps)),
            pl.BlockSpec((tm, LANES), lambda i, j: (i, 0)),
        ],
        out_shape=[
            jax.ShapeDtypeStruct((t, N_ROWOUT), BF16),
            jax.ShapeDtypeStruct((bsz, N_TOUT, s_len), BF16),
            jax.ShapeDtypeStruct((t, LANES), F32),
        ],
        scratch_shapes=[pltpu.VMEM((tm, d), BF16)],
        compiler_params=pltpu.CompilerParams(
            dimension_semantics=("arbitrary", "arbitrary"), vmem_limit_bytes=VMEM_LIMIT),
        name="inproj",
    )(x2, g, w_row, w_t, w_f)


def _rnn_body(xr_ref, gr_ref, cw_ref, cb_ref, wr_ref, br_ref, wi_ref, bi_ref, ap_ref, y_ref, a_scr, u_scr):
    s, width = xr_ref.shape
    nblk = width // RNN_BLOCK_W
    half = pl.program_id(1)
    row = lax.broadcasted_iota(jnp.int32, (s, RNN_BLOCK_W), 0)
    for n in range(nblk):
        sl = slice(n * RNN_BLOCK_W, (n + 1) * RNN_BLOCK_W)
        x = xr_ref[:, sl].astype(F32)
        x1 = jnp.where(row >= 1, pltpu.roll(x, 1, 0), 0.0)
        x2 = jnp.where(row >= 2, pltpu.roll(x, 2, 0), 0.0)
        x3 = jnp.where(row >= 3, pltpu.roll(x, 3, 0), 0.0)
        xc = (cb_ref[:, sl] + x3 * cw_ref[0:1, sl] + x2 * cw_ref[1:2, sl]
              + x1 * cw_ref[2:3, sl] + x * cw_ref[3:4, sl])
        xb = xc.astype(BF16)
        blk = half * nblk + n
        r = jax.nn.sigmoid(jnp.dot(xb, wr_ref[blk], preferred_element_type=F32) + br_ref[:, sl])
        gi = jax.nn.sigmoid(jnp.dot(xb, wi_ref[blk], preferred_element_type=F32) + bi_ref[:, sl])
        log_a = (-RG_LRU_C * r) * _softplus(-ap_ref[:, sl])
        a = jnp.exp(log_a)
        a_scr[:, sl] = a
        u_scr[:, sl] = jnp.sqrt(1.0 - a * a) * (gi * xc)

    row8 = lax.broadcasted_iota(jnp.int32, (SUBLANES, width), 0)

    def group(k, hprev):
        off = pl.multiple_of(k * SUBLANES, SUBLANES)
        a = a_scr[pl.ds(off, SUBLANES), :]
        u = u_scr[pl.ds(off, SUBLANES), :]
        for d in (1, 2, 4):
            a_sh = jnp.where(row8 >= d, pltpu.roll(a, d, 0), 1.0)
            u_sh = jnp.where(row8 >= d, pltpu.roll(u, d, 0), 0.0)
            u = a * u_sh + u
            a = a * a_sh
        h = a * hprev + u
        u_scr[pl.ds(off, SUBLANES), :] = h
        return h[SUBLANES - 1:SUBLANES, :]

    lax.fori_loop(0, s // SUBLANES, group, jnp.zeros((1, width), F32))

    for n in range(nblk):
        sl = slice(n * RNN_BLOCK_W, (n + 1) * RNN_BLOCK_W)
        y_ref[:, sl] = (jax.nn.gelu(gr_ref[:, sl].astype(F32)) * u_scr[:, sl]).astype(y_ref.dtype)


def _rnn(main3, conv_w, conv_b, w_r, b_r, w_i, b_i, a_param):
    b, s, _ = main3.shape
    w = RNN_HALF
    nh = D_RNN // w
    vec = lambda k: pl.BlockSpec((k, w), lambda i, j: (0, j))
    full3 = pl.BlockSpec((RNN_BLOCKS, RNN_BLOCK_W, RNN_BLOCK_W), lambda i, j: (0, 0, 0))
    return pl.pallas_call(
        _rnn_body,
        grid=(b, nh),
        in_specs=[
            pl.BlockSpec((None, s, w), lambda i, j: (i, 0, COL_XRNN // w + j)),
            pl.BlockSpec((None, s, w), lambda i, j: (i, 0, COL_GRNN // w + j)),
            vec(4), vec(1), full3, vec(1), full3, vec(1), vec(1),
        ],
        out_specs=pl.BlockSpec((None, s, w), lambda i, j: (i, 0, j)),
        out_shape=jax.ShapeDtypeStruct((b, s, D_RNN), BF16),
        scratch_shapes=[pltpu.VMEM((s, w), F32), pltpu.VMEM((s, w), F32)],
        compiler_params=pltpu.CompilerParams(
            dimension_semantics=("arbitrary", "arbitrary"), vmem_limit_bytes=VMEM_LIMIT),
        name="rnn",
    )(main3, main3, conv_w, conv_b, w_r, b_r, w_i, b_i, a_param)


def _cum_body(flog_ref, fb_ref, qaugt_ref, kaug_ref):
    s = flog_ref.shape[0]
    z = flog_ref[...] + fb_ref[...]
    c = -_softplus(-z) * LOG2E
    row = lax.broadcasted_iota(jnp.int32, c.shape, 0)
    d = 1
    while d < s:
        c = c + jnp.where(row >= d, pltpu.roll(c, d, 0), 0.0)
        d *= 2
    hi = c.astype(BF16).astype(F32)
    mid = (c - hi).astype(BF16).astype(F32)
    lo = (c - hi - mid).astype(BF16).astype(F32)
    slot = lax.broadcasted_iota(jnp.int32, c.shape, 1) % AUG_GROUP
    part = jnp.where(slot % AUG_PARTS == 0, hi, jnp.where(slot % AUG_PARTS == 1, mid, lo))
    used = slot < AUG_SLOTS
    qside = slot < AUG_PARTS
    qaug = jnp.where(used, jnp.where(qside, part, 1.0), 0.0)
    kaug = jnp.where(used, jnp.where(qside, 1.0, -part), 0.0)
    qaugt_ref[...] = qaug.T.astype(BF16)
    kaug_ref[...] = kaug.astype(BF16)


def _cum(flog3, fb):
    b, s, _ = flog3.shape
    return pl.pallas_call(
        _cum_body,
        grid=(b,),
        in_specs=[
            pl.BlockSpec((None, s, LANES), lambda i: (i, 0, 0)),
            pl.BlockSpec((1, LANES), lambda i: (0, 0)),
        ],
        out_specs=[
            pl.BlockSpec((None, LANES, s), lambda i: (i, 0, 0)),
            pl.BlockSpec((None, s, LANES), lambda i: (i, 0, 0)),
        ],
        out_shape=[
            jax.ShapeDtypeStruct((b, LANES, s), BF16),
            jax.ShapeDtypeStruct((b, s, LANES), BF16),
        ],
        compiler_params=pltpu.CompilerParams(dimension_semantics=("arbitrary",)),
        name="fox_cumsum",
    )(flog3, fb)


def _attend(q_tile, k_tile, vt_tile, n_far, near, sc_scr, p_scr, acc_scr):
    n_tiles = n_far + len(near)
    assert [j for j, _ in near] == list(range(n_far, n_tiles))
    maps = range(2)
    tq = sc_scr.shape[-1]

    def scores(t, slot):
        for w in maps:
            sc_scr[slot, w] = jnp.dot(k_tile(t, w), q_tile(w), preferred_element_type=F32)

    def values(t, slot, alphas):
        return tuple(alphas[w] * acc_scr[w] + jnp.dot(vt_tile(t, w), p_scr[slot, w], preferred_element_type=F32)
                     for w in maps)

    def iteration(t, slot, ms, alphas, bias, has_prev, has_next):
        if has_prev:
            accs = values(t - 1, 1 - slot, alphas)
            for w in maps:
                acc_scr[w] = accs[w]
        if has_next:
            scores(t + 1, 1 - slot)
        new_ms, new_alphas = [], []
        for w in maps:
            sc = sc_scr[slot, w] if bias is None else sc_scr[slot, w] + bias
            m_new = jnp.maximum(ms[w], jnp.max(sc, axis=0, keepdims=True))
            new_alphas.append(jnp.exp2(ms[w] - m_new))
            p_scr[slot, w] = jnp.exp2(sc - m_new).astype(BF16)
            new_ms.append(m_new)
        return tuple(new_ms), tuple(new_alphas)

    for w in maps:
        acc_scr[w] = jnp.zeros(acc_scr.shape[1:], F32)
    ms = (jnp.full((1, tq), NEG, F32),) * 2
    bias_of = lambda t: None if t < n_far else near[t - n_far][1]
    scores(0, 0)
    ms, alphas = iteration(0, 0, ms, None, bias_of(0), False, n_tiles > 1)
    t = 1
    n_pairs = max(n_far - 1, 0) // 2
    if n_pairs > 0:
        def pair(k, st):
            st = iteration(1 + 2 * k, 1, st[0], st[1], None, True, True)
            return iteration(2 + 2 * k, 0, st[0], st[1], None, True, True)
        ms, alphas = lax.fori_loop(0, n_pairs, pair, (ms, alphas))
        t = 1 + 2 * n_pairs
    while t < n_tiles:
        ms, alphas = iteration(t, t % 2, ms, alphas, bias_of(t), True, t + 1 < n_tiles)
        t += 1
    return values(n_tiles - 1, (n_tiles - 1) % 2, alphas)


def _diff_body(rel_ref, qt_ref, k_ref, vt_ref, lq1_ref, lk1_ref, lq2_ref, lk2_ref, sg_ref, o_ref,
               bias_scr, qa_scr, qb_scr, vx_scr, sc_scr, p_scr, acc_scr, *, lam_init):
    s = k_ref.shape[0]
    head = pl.program_id(0)

    @pl.when(pl.program_id(1) == 0)
    def _():
        ki = lax.broadcasted_iota(jnp.int32, (TK, TQ), 0)
        qi = lax.broadcasted_iota(jnp.int32, (TK, TQ), 1)
        far = rel_ref[N_BUCKETS - 1, head]
        for idx in range(3):
            dist = qi - ki + (1 - idx) * TK
            n = jnp.maximum(dist, 0)
            nf = jnp.maximum(n, 1).astype(F32)
            large = MAX_EXACT + (jnp.log(nf / MAX_EXACT) / math.log(MAX_DISTANCE / MAX_EXACT)
                                 * (N_BUCKETS - MAX_EXACT)).astype(jnp.int32)
            large = jnp.minimum(large, N_BUCKETS - 1)
            bucket = jnp.where(n < MAX_EXACT, n, large)
            bias = jnp.zeros((TK, TQ), F32)
            for bk in range(N_BUCKETS):
                bias = jnp.where(bucket == bk, rel_ref[bk, head], bias)
            bias_scr[idx] = jnp.where(dist >= 0, (bias - far) * LOG2E, NEG)

    rowi = lax.broadcasted_iota(jnp.int32, qt_ref.shape, 0)
    qt = qt_ref[...].astype(F32)
    qa_scr[...] = jnp.where(rowi < DIFF_HEAD_DIM, qt, 0.0).astype(BF16)
    qb_scr[...] = jnp.where(rowi < DIFF_HEAD_DIM, 0.0, qt).astype(BF16)
    vx_scr[0:LANES, :] = vt_ref[...]
    vx_scr[LANES:, :] = jnp.ones((BF16_ROWS, s), BF16)

    lam = (jnp.exp(jnp.sum(lq1_ref[...] * lk1_ref[...], axis=-1, keepdims=True))
           - jnp.exp(jnp.sum(lq2_ref[...] * lk2_ref[...], axis=-1, keepdims=True)) + lam_init)

    k_tile = lambda j, w: k_ref[pl.ds(j * TK, TK), :]
    vt_tile = lambda j, w: vx_scr[:, pl.ds(j * TK, TK)]
    for i in range(s // TQ):
        qsl = slice(i * TQ, (i + 1) * TQ)
        j0 = i * (TQ // TK)
        near = [(j0 - 1 + idx, bias_scr[idx]) for idx in range(3) if j0 - 1 + idx >= 0]
        n_far = max(j0 - 1, 0)
        q_tile = lambda w, qsl=qsl: (qa_scr, qb_scr)[w][:, qsl]
        acca, accb = _attend(q_tile, k_tile, vt_tile, n_far, near, sc_scr, p_scr, acc_scr)
        o = (acca[:LANES] / acca[LANES:LANES + 1] - lam * (accb[:LANES] / accb[LANES:LANES + 1]))
        o = o * lax.rsqrt(jnp.mean(o * o, axis=0, keepdims=True) + EPS)
        o_ref[qsl, :] = (o.T * sg_ref[...] * (1.0 - lam_init)).astype(o_ref.dtype)


def _diff_attn(main3, maint, rel_bias, lq1, lk1, lq2, lk2, subln_g, lam_init):
    b, s, _ = main3.shape
    small = lambda w: pl.BlockSpec((1, w), lambda h, i: (0, 0))
    trow = lambda base: pl.BlockSpec((None, LANES, s), lambda h, i: (i, base // LANES + h, 0))
    return pl.pallas_call(
        functools.partial(_diff_body, lam_init=lam_init),
        grid=(DIFF_HEADS, b),
        in_specs=[
            pl.BlockSpec(memory_space=pltpu.SMEM),
            trow(ROW_DQ),
            pl.BlockSpec((None, s, LANES), lambda h, i: (i, 0, COL_DK // LANES + h)),
            trow(ROW_DV),
            small(DIFF_HEAD_DIM), small(DIFF_HEAD_DIM), small(DIFF_HEAD_DIM), small(DIFF_HEAD_DIM),
            small(2 * DIFF_HEAD_DIM),
        ],
        out_specs=pl.BlockSpec((None, s, LANES), lambda h, i: (i, 0, h)),
        out_shape=jax.ShapeDtypeStruct((b, s, DIFF_WIDTH), BF16),
        scratch_shapes=[
            pltpu.VMEM((3, TK, TQ), F32),
            pltpu.VMEM((LANES, s), BF16),
            pltpu.VMEM((LANES, s), BF16),
            pltpu.VMEM((LANES + BF16_ROWS, s), BF16),
            pltpu.VMEM((2, 2, TK, TQ), F32),
            pltpu.VMEM((2, 2, TK, TQ), BF16),
            pltpu.VMEM((2, LANES + BF16_ROWS, TQ), F32),
        ],
        compiler_params=pltpu.CompilerParams(
            dimension_semantics=("arbitrary", "arbitrary"), vmem_limit_bytes=VMEM_LIMIT),
        name="diff_attn",
    )(rel_bias, maint, main3, maint, lq1, lk1, lq2, lk2, subln_g)


def _fox_body(qt_ref, k_ref, vt_ref, qaugt_ref, kaug_ref, o_ref,
              mask_scr, qa_scr, qb_scr, ka_scr, kb_scr, va_scr, vb_scr, sc_scr, p_scr, acc_scr):
    s = k_ref.shape[0]
    pair = pl.program_id(0)

    @pl.when(pl.program_id(1) == 0)
    def _():
        ki = lax.broadcasted_iota(jnp.int32, (TK, TQ), 0)
        qi = lax.broadcasted_iota(jnp.int32, (TK, TQ), 1)
        for idx in range(TQ // TK):
            mask_scr[idx] = jnp.where(qi >= ki + idx * TK, 0.0, NEG)

    half = FOX_HEAD_DIM
    lo_a, lo_b = half + AUG_GROUP * pair, AUG_GROUP * pair

    def split(x, aug, idx):
        is_a = idx < half
        aug_a = jnp.where((idx >= lo_a) & (idx < lo_a + AUG_SLOTS), aug, 0.0)
        aug_b = jnp.where((idx >= lo_b) & (idx < lo_b + AUG_SLOTS), aug, 0.0)
        return jnp.where(is_a, x, aug_a).astype(BF16), jnp.where(is_a, aug_b, x).astype(BF16)

    rowi = lax.broadcasted_iota(jnp.int32, qt_ref.shape, 0)
    lanei = lax.broadcasted_iota(jnp.int32, k_ref.shape, 1)
    qa_scr[...], qb_scr[...] = split(qt_ref[...].astype(F32), qaugt_ref[...].astype(F32), rowi)
    ka_scr[...], kb_scr[...] = split(k_ref[...].astype(F32), kaug_ref[...].astype(F32), lanei)
    vt = vt_ref[...].astype(F32)
    va_scr[...] = jnp.where(rowi < half, vt, 1.0).astype(BF16)
    vb_scr[...] = jnp.where(rowi < half, 1.0, vt).astype(BF16)

    k_tile = lambda j, w: (ka_scr, kb_scr)[w][pl.ds(j * TK, TK), :]
    vt_tile = lambda j, w: (va_scr, vb_scr)[w][:, pl.ds(j * TK, TK)]
    rows = lax.broadcasted_iota(jnp.int32, (LANES, TQ), 0)
    for i in range(s // TQ):
        qsl = slice(i * TQ, (i + 1) * TQ)
        j0 = i * (TQ // TK)
        near = [(j0 + idx, mask_scr[idx]) for idx in range(TQ // TK)]
        q_tile = lambda w, qsl=qsl: (qa_scr, qb_scr)[w][:, qsl]
        acca, accb = _attend(q_tile, k_tile, vt_tile, j0, near, sc_scr, p_scr, acc_scr)
        o = jnp.where(rows < half, acca / acca[half:half + 1], accb / accb[0:1])
        o_ref[qsl, :] = o.T.astype(o_ref.dtype)


def _fox_attn(main3, maint, qaugt, kaug):
    b, s, _ = main3.shape
    trow = lambda base: pl.BlockSpec((None, LANES, s), lambda p, i: (i, base // LANES + p, 0))
    tsc = pltpu.VMEM((LANES, s), BF16)
    return pl.pallas_call(
        _fox_body,
        grid=(FOX_HEADS // 2, b),
        in_specs=[
            trow(ROW_FQ),
            pl.BlockSpec((None, s, LANES), lambda p, i: (i, 0, COL_FK // LANES + p)),
            trow(ROW_FV),
            pl.BlockSpec((None, LANES, s), lambda p, i: (i, 0, 0)),
            pl.BlockSpec((None, s, LANES), lambda p, i: (i, 0, 0)),
        ],
        out_specs=pl.BlockSpec((None, s, LANES), lambda p, i: (i, 0, p)),
        out_shape=jax.ShapeDtypeStruct((b, s, FOX_WIDTH), BF16),
        scratch_shapes=[
            pltpu.VMEM((TQ // TK, TK, TQ), F32),
            tsc, tsc, pltpu.VMEM((s, LANES), BF16), pltpu.VMEM((s, LANES), BF16), tsc, tsc,
            pltpu.VMEM((2, 2, TK, TQ), F32),
            pltpu.VMEM((2, 2, TK, TQ), BF16),
            pltpu.VMEM((2, LANES, TQ), F32),
        ],
        compiler_params=pltpu.CompilerParams(
            dimension_semantics=("arbitrary", "arbitrary"), vmem_limit_bytes=VMEM_LIMIT),
        name="fox_attn",
    )(maint, main3, maint, qaugt, kaug)


def _merge_body(x_ref, g0_ref, g1_ref, g2_ref, gb_ref, yr_ref, yd_ref, yf_ref, wr_ref, wd_ref, wf_ref, wo_ref,
                o_ref):
    def branch(g_ref, k, y_ref, w_ref):
        gate = jax.nn.sigmoid(g_ref[...].astype(F32) + gb_ref[k:k + 1, :])
        return gate * jnp.dot(y_ref[...], w_ref[...], preferred_element_type=F32)

    m = branch(g0_ref, 0, yr_ref, wr_ref) + branch(g1_ref, 1, yd_ref, wd_ref) + branch(g2_ref, 2, yf_ref, wf_ref)
    o_ref[...] = x_ref[...] + jnp.dot(m.astype(BF16), wo_ref[...], preferred_element_type=F32)


def _merge(x2, main2, gate_b, y_rnn, y_diff, y_fox, w_r, w_d, w_f, w_o):
    t, d = x2.shape
    tm = min(TM_MERGE, t)
    rows = lambda w, c=0: pl.BlockSpec((tm, w), lambda i: (i, c))
    whole = lambda a: pl.BlockSpec(a.shape, lambda i: (0, 0))
    g0 = COL_GATES // d
    return pl.pallas_call(
        _merge_body,
        grid=(t // tm,),
        in_specs=[
            rows(d), rows(d, g0), rows(d, g0 + 1), rows(d, g0 + 2), whole(gate_b),
            rows(D_RNN), rows(DIFF_WIDTH), rows(FOX_WIDTH),
            whole(w_r), whole(w_d), whole(w_f), whole(w_o),
        ],
        out_specs=rows(d),
        out_shape=jax.ShapeDtypeStruct((t, d), F32),
        compiler_params=pltpu.CompilerParams(dimension_semantics=("arbitrary",), vmem_limit_bytes=VMEM_LIMIT),
        name="merge",
    )(x2, main2, main2, main2, gate_b, y_rnn, y_diff, y_fox, w_r, w_d, w_f, w_o)


def _ffn_body(x_ref, g_ref, wg_ref, wv_ref, cwg_ref, cwv_ref, cbg_ref, cbv_ref, wd_ref, o_ref,
              h_scr, acc_scr, carry_g, carry_v, *, tiles_per_seq):
    i = pl.program_id(0)
    c = pl.program_id(1)
    nc = pl.num_programs(1)
    tm = x_ref.shape[0]
    cw = wg_ref.shape[1]

    @pl.when(c == 0)
    def _():
        h_scr[...] = _rms(x_ref[...], g_ref[...]).astype(BF16)

    row = lax.broadcasted_iota(jnp.int32, (tm, cw), 0)

    @pl.when((i % tiles_per_seq) == 0)
    def _():
        carry_g[c] = jnp.zeros((SUBLANES, cw), F32)
        carry_v[c] = jnp.zeros((SUBLANES, cw), F32)

    def conv(w_ref, cw_ref, cb_ref, carry_ref):
        u = jnp.dot(h_scr[...], w_ref[...], preferred_element_type=F32)
        prev = carry_ref[c]
        carry_ref[c] = u[tm - SUBLANES:, :]
        u1 = jnp.where(row == 0, prev[SUBLANES - 1:, :], pltpu.roll(u, 1, 0))
        u2 = jnp.where(row == 0, prev[SUBLANES - 2:SUBLANES - 1, :],
                       jnp.where(row == 1, prev[SUBLANES - 1:, :], pltpu.roll(u, 2, 0)))
        return cb_ref[...] + u2 * cw_ref[0:1, :] + u1 * cw_ref[1:2, :] + u * cw_ref[2:3, :]

    ug = conv(wg_ref, cwg_ref, cbg_ref, carry_g)
    uv = conv(wv_ref, cwv_ref, cbv_ref, carry_v)
    act = (jax.nn.gelu(ug) * uv).astype(BF16)
    part = jnp.dot(act, wd_ref[...], preferred_element_type=F32)

    @pl.when(c == 0)
    def _():
        acc_scr[...] = x_ref[...] + part

    @pl.when(c > 0)
    def _():
        acc_scr[...] += part

    @pl.when(c == nc - 1)
    def _():
        o_ref[...] = acc_scr[...]


def _ffn(x2, g, w_up, conv_w, conv_b, w_down, seq_len):
    t, d = x2.shape
    tm = min(TM_FFN, seq_len)
    cw = FF_CHUNK
    nc = D_FF // cw
    return pl.pallas_call(
        functools.partial(_ffn_body, tiles_per_seq=seq_len // tm),
        grid=(t // tm, nc),
        in_specs=[
            pl.BlockSpec((tm, d), lambda i, c: (i, 0)),
            pl.BlockSpec((1, d), lambda i, c: (0, 0)),
            pl.BlockSpec((d, cw), lambda i, c: (0, c)),
            pl.BlockSpec((d, cw), lambda i, c: (0, nc + c)),
            pl.BlockSpec((3, cw), lambda i, c: (0, c)),
            pl.BlockSpec((3, cw), lambda i, c: (0, nc + c)),
            pl.BlockSpec((1, cw), lambda i, c: (0, c)),
            pl.BlockSpec((1, cw), lambda i, c: (0, nc + c)),
            pl.BlockSpec((cw, d), lambda i, c: (c, 0)),
        ],
        out_specs=pl.BlockSpec((tm, d), lambda i, c: (i, 0)),
        out_shape=jax.ShapeDtypeStruct((t, d), F32),
        scratch_shapes=[
            pltpu.VMEM((tm, d), BF16),
            pltpu.VMEM((tm, d), F32),
            pltpu.VMEM((nc, SUBLANES, cw), F32),
            pltpu.VMEM((nc, SUBLANES, cw), F32),
        ],
        compiler_params=pltpu.CompilerParams(
            dimension_semantics=("arbitrary", "arbitrary"), vmem_limit_bytes=VMEM_LIMIT),
        name="ffn",
    )(x2, g, w_up, w_up, conv_w, conv_w, conv_b, conv_b, w_down)


def _final_norm_body(x_ref, g_ref, o_ref):
    o_ref[...] = _rms(x_ref[...], g_ref[...])


def _final_norm(x2, g):
    t, d = x2.shape
    tm = min(TM_PROJ, t)
    return pl.pallas_call(
        _final_norm_body,
        grid=(t // tm,),
        in_specs=[pl.BlockSpec((tm, d), lambda i: (i, 0)), pl.BlockSpec((1, d), lambda i: (0, 0))],
        out_specs=pl.BlockSpec((tm, d), lambda i: (i, 0)),
        out_shape=jax.ShapeDtypeStruct((t, d), F32),
        compiler_params=pltpu.CompilerParams(dimension_semantics=("arbitrary",)),
        name="final_norm",
    )(x2, g)


def _aug_head_of_lane():
    heads = []
    for lane in range(LANES):
        group, slot = (lane % (LANES // 2)) // AUG_GROUP, lane % AUG_GROUP
        if group < FOX_HEADS // 2 and slot < AUG_SLOTS:
            heads.append(2 * group + (1 if lane < LANES // 2 else 0))
        else:
            heads.append(-1)
    return heads


def _prep_w_in(w_in_l, fox_b_f_l):
    seg = lambda a, n: w_in_l[:, a:a + n]
    w_row = jnp.concatenate([seg(W_XRNN, D_RNN), seg(W_GRNN, D_RNN), seg(W_DK, DIFF_QK), seg(W_FK, FOX_WIDTH),
                             seg(W_GATES, N_BRANCH * D_MODEL)], axis=1)
    w_t = jnp.concatenate([seg(W_DQ, DIFF_QK) * (DIFF_HEAD_DIM ** -0.5 * LOG2E),
                           seg(W_FQ, FOX_WIDTH) * (FOX_HEAD_DIM ** -0.5 * LOG2E),
                           seg(W_DV, DIFF_WIDTH), seg(W_FV, FOX_WIDTH)], axis=1).T
    heads = _aug_head_of_lane()
    idx = jnp.array([max(h, 0) for h in heads], jnp.int32)
    live = jnp.array([1.0 if h >= 0 else 0.0 for h in heads], F32)
    w_f = seg(W_FLOG, FOX_HEADS)[:, idx] * live
    fb = (fox_b_f_l[idx] * live).reshape(1, LANES)
    return w_row.astype(BF16), w_t.astype(BF16), w_f.astype(BF16), fb


def kernel(x, norm1_g, w_in, rnn_conv_w, rnn_conv_b, rg_w_r, rg_b_r, rg_w_i, rg_b_i, rg_a,
           diff_lq1, diff_lk1, diff_lq2, diff_lk2, diff_subln_g, rel_bias, fox_b_f, gate_b,
           w_br_rnn, w_br_diff, w_br_fox, w_out, norm2_g, ffn_up, ffn_conv_w, ffn_conv_b,
           ffn_down, final_g):
    bsz, s_len, d = x.shape
    t = bsz * s_len
    x2 = x.reshape(t, d)
    row = lambda v: v.reshape(1, -1)
    for l in range(DEPTH):
        w_row, w_t, w_f, fb = _prep_w_in(w_in[l], fox_b_f[l])
        main2, maint, flog = _inproj(x2, row(norm1_g[l]), w_row, w_t, w_f, bsz, s_len)
        main3 = main2.reshape(bsz, s_len, N_ROWOUT)

        y_rnn = _rnn(main3, rnn_conv_w[l], row(rnn_conv_b[l]), rg_w_r[l].astype(BF16), row(rg_b_r[l]),
                     rg_w_i[l].astype(BF16), row(rg_b_i[l]), row(rg_a[l]))

        lam_init = 0.8 - 0.6 * math.exp(-0.3 * l)
        y_diff = _diff_attn(main3, maint, rel_bias, row(diff_lq1[l]), row(diff_lk1[l]), row(diff_lq2[l]),
                            row(diff_lk2[l]), row(diff_subln_g[l]), lam_init)

        qaugt, kaug = _cum(flog.reshape(bsz, s_len, LANES), fb)
        y_fox = _fox_attn(main3, maint, qaugt, kaug)

        x2 = _merge(x2, main2, gate_b[l], y_rnn.reshape(t, D_RNN), y_diff.reshape(t, DIFF_WIDTH),
                    y_fox.reshape(t, FOX_WIDTH), w_br_rnn[l].astype(BF16), w_br_diff[l].astype(BF16),
                    w_br_fox[l].astype(BF16), w_out[l].astype(BF16))

        x2 = _ffn(x2, row(norm2_g[l]), ffn_up[l].astype(BF16), ffn_conv_w[l], row(ffn_conv_b[l]),
                  ffn_down[l].astype(BF16), s_len)
    return _final_norm(x2, row(final_g)).reshape(bsz, s_len, d)
```

```python
import functools
import math

import jax
import jax.numpy as jnp
from jax import lax
from jax.experimental import pallas as pl
from jax.experimental.pallas import tpu as pltpu

F32 = jnp.float32
BF16 = jnp.bfloat16

D_MODEL = 1024
DEPTH = 2
D_RNN = D_MODEL
RNN_BLOCKS = 8
RNN_BLOCK_W = D_RNN // RNN_BLOCKS
RG_LRU_C = 8.0
DIFF_HEADS = 4
DIFF_HEAD_DIM = 64
DIFF_QK = DIFF_HEADS * 2 * DIFF_HEAD_DIM
DIFF_WIDTH = DIFF_HEADS * 2 * DIFF_HEAD_DIM
FOX_HEADS = 8
FOX_HEAD_DIM = 64
FOX_WIDTH = FOX_HEADS * FOX_HEAD_DIM
N_BUCKETS = 32
MAX_EXACT = N_BUCKETS // 2
MAX_DISTANCE = 128
D_FF = ((8 * D_MODEL // 3 + 127) // 128) * 128
N_BRANCH = 3
EPS = 1e-6
LOG2E = 1.4426950408889634
NEG = -1e30

LANES = 128
SUBLANES = 8
BF16_ROWS = 16

W_XRNN = 0
W_GRNN = W_XRNN + D_RNN
W_DQ = W_GRNN + D_RNN
W_DK = W_DQ + DIFF_QK
W_DV = W_DK + DIFF_QK
W_FQ = W_DV + DIFF_WIDTH
W_FK = W_FQ + FOX_WIDTH
W_FV = W_FK + FOX_WIDTH
W_FLOG = W_FV + FOX_WIDTH
W_GATES = W_FLOG + FOX_HEADS

COL_XRNN = 0
COL_GRNN = COL_XRNN + D_RNN
COL_DK = COL_GRNN + D_RNN
COL_FK = COL_DK + DIFF_QK
COL_GATES = COL_FK + FOX_WIDTH
N_ROWOUT = COL_GATES + N_BRANCH * D_MODEL
ROW_DQ = 0
ROW_FQ = ROW_DQ + DIFF_QK
ROW_DV = ROW_FQ + FOX_WIDTH
ROW_FV = ROW_DV + DIFF_WIDTH
N_TOUT = ROW_FV + FOX_WIDTH

TQ = 512
TK = 256
TM_PROJ = 512
TN_PROJ = 1024
TM_MERGE = 512
TM_FFN = 512
FF_SUB = 256
FF_ROWS = 64
RNN_HALF = D_RNN // 2
VMEM_LIMIT = 56 * 1024 * 1024

AUG_GROUP = 8
AUG_SLOTS = 6
AUG_PARTS = 3


def _softplus(z):
    return jnp.maximum(z, 0.0) + jnp.log1p(jnp.exp(-jnp.abs(z)))


def _rms(x, g):
    return x * lax.rsqrt(jnp.mean(x * x, axis=-1, keepdims=True) + EPS) * g


def _inproj_body(x_ref, g_ref, w_ref, wt_ref, wf_ref, out_ref, outt_ref, flog_ref, h_scr):
    h_scr[...] = _rms(x_ref[...], g_ref[...]).astype(BF16)
    flog_ref[...] = jnp.dot(h_scr[...], wf_ref[...], preferred_element_type=F32)
    for j in range(w_ref.shape[1] // TN_PROJ):
        cols = slice(j * TN_PROJ, (j + 1) * TN_PROJ)
        out_ref[:, cols] = jnp.dot(h_scr[...], w_ref[:, cols], preferred_element_type=F32).astype(out_ref.dtype)
    for j in range(wt_ref.shape[0] // TN_PROJ):
        rows = slice(j * TN_PROJ, (j + 1) * TN_PROJ)
        outt_ref[rows, :] = lax.dot_general(wt_ref[rows, :], h_scr[...], (((1,), (1,)), ((), ())),
                                            preferred_element_type=F32).astype(outt_ref.dtype)


def _resident(shape):
    return pl.BlockSpec(shape, lambda *_: (0,) * len(shape), pipeline_mode=pl.Buffered(1))


def _inproj(x2, g, w_row, w_t, w_f, bsz, s_len):
    t, d = x2.shape
    tm = min(TM_PROJ, s_len)
    tps = s_len // tm
    return pl.pallas_call(
        _inproj_body,
        grid=(t // tm,),
        in_specs=[
            pl.BlockSpec((tm, d), lambda i: (i, 0)),
            _resident(g.shape), _resident(w_row.shape), _resident(w_t.shape), _resident(w_f.shape),
        ],
        out_specs=[
            pl.BlockSpec((tm, N_ROWOUT), lambda i: (i, 0)),
            pl.BlockSpec((None, N_TOUT, tm), lambda i: (i // tps, 0, i % tps)),
            pl.BlockSpec((tm, LANES), lambda i: (i, 0)),
        ],
        out_shape=[
            jax.ShapeDtypeStruct((t, N_ROWOUT), BF16),
            jax.ShapeDtypeStruct((bsz, N_TOUT, s_len), BF16),
            jax.ShapeDtypeStruct((t, LANES), F32),
        ],
        scratch_shapes=[pltpu.VMEM((tm, d), BF16)],
        compiler_params=pltpu.CompilerParams(dimension_semantics=("arbitrary",), vmem_limit_bytes=VMEM_LIMIT),
        name="inproj",
    )(x2, g, w_row, w_t, w_f)


def _rnn_body(xr_ref, gr_ref, cw_ref, cb_ref, wr_ref, br_ref, wi_ref, bi_ref, ap_ref, y_ref, a_scr, u_scr):
    s, width = xr_ref.shape
    nblk = width // RNN_BLOCK_W
    half = pl.program_id(1)
    row = lax.broadcasted_iota(jnp.int32, (s, RNN_BLOCK_W), 0)
    for n in range(nblk):
        sl = slice(n * RNN_BLOCK_W, (n + 1) * RNN_BLOCK_W)
        x = xr_ref[:, sl].astype(F32)
        x1 = jnp.where(row >= 1, pltpu.roll(x, 1, 0), 0.0)
        x2 = jnp.where(row >= 2, pltpu.roll(x, 2, 0), 0.0)
        x3 = jnp.where(row >= 3, pltpu.roll(x, 3, 0), 0.0)
        xc = (cb_ref[:, sl] + x3 * cw_ref[0:1, sl] + x2 * cw_ref[1:2, sl]
              + x1 * cw_ref[2:3, sl] + x * cw_ref[3:4, sl])
        xb = xc.astype(BF16)
        blk = half * nblk + n
        r = jax.nn.sigmoid(jnp.dot(xb, wr_ref[blk], preferred_element_type=F32) + br_ref[:, sl])
        gi = jax.nn.sigmoid(jnp.dot(xb, wi_ref[blk], preferred_element_type=F32) + bi_ref[:, sl])
        log_a = (-RG_LRU_C * r) * _softplus(-ap_ref[:, sl])
        a = jnp.exp(log_a)
        a_scr[:, sl] = a
        u_scr[:, sl] = jnp.sqrt(1.0 - a * a) * (gi * xc)

    row8 = lax.broadcasted_iota(jnp.int32, (SUBLANES, width), 0)

    def group(k, hprev):
        off = pl.multiple_of(k * SUBLANES, SUBLANES)
        a = a_scr[pl.ds(off, SUBLANES), :]
        u = u_scr[pl.ds(off, SUBLANES), :]
        for d in (1, 2, 4):
            a_sh = jnp.where(row8 >= d, pltpu.roll(a, d, 0), 1.0)
            u_sh = jnp.where(row8 >= d, pltpu.roll(u, d, 0), 0.0)
            u = a * u_sh + u
            a = a * a_sh
        h = a * hprev + u
        u_scr[pl.ds(off, SUBLANES), :] = h
        return h[SUBLANES - 1:SUBLANES, :]

    lax.fori_loop(0, s // SUBLANES, group, jnp.zeros((1, width), F32))

    for n in range(nblk):
        sl = slice(n * RNN_BLOCK_W, (n + 1) * RNN_BLOCK_W)
        y_ref[:, sl] = (jax.nn.gelu(gr_ref[:, sl].astype(F32)) * u_scr[:, sl]).astype(y_ref.dtype)


def _rnn(main3, conv_w, conv_b, w_r, b_r, w_i, b_i, a_param):
    b, s, _ = main3.shape
    w = RNN_HALF
    nh = D_RNN // w
    vec = lambda k: pl.BlockSpec((k, w), lambda i, j: (0, j))
    full3 = pl.BlockSpec((RNN_BLOCKS, RNN_BLOCK_W, RNN_BLOCK_W), lambda i, j: (0, 0, 0))
    return pl.pallas_call(
        _rnn_body,
        grid=(b, nh),
        in_specs=[
            pl.BlockSpec((None, s, w), lambda i, j: (i, 0, COL_XRNN // w + j)),
            pl.BlockSpec((None, s, w), lambda i, j: (i, 0, COL_GRNN // w + j)),
            vec(4), vec(1), full3, vec(1), full3, vec(1), vec(1),
        ],
        out_specs=pl.BlockSpec((None, s, w), lambda i, j: (i, 0, j)),
        out_shape=jax.ShapeDtypeStruct((b, s, D_RNN), BF16),
        scratch_shapes=[pltpu.VMEM((s, w), F32), pltpu.VMEM((s, w), F32)],
        compiler_params=pltpu.CompilerParams(
            dimension_semantics=("arbitrary", "arbitrary"), vmem_limit_bytes=VMEM_LIMIT),
        name="rnn",
    )(main3, main3, conv_w, conv_b, w_r, b_r, w_i, b_i, a_param)


def _cum_body(flog_ref, fb_ref, qaugt_ref, kaug_ref):
    s = flog_ref.shape[0]
    z = flog_ref[...] + fb_ref[...]
    c = -_softplus(-z) * LOG2E
    row = lax.broadcasted_iota(jnp.int32, c.shape, 0)
    d = 1
    while d < s:
        c = c + jnp.where(row >= d, pltpu.roll(c, d, 0), 0.0)
        d *= 2
    hi = c.astype(BF16).astype(F32)
    mid = (c - hi).astype(BF16).astype(F32)
    lo = (c - hi - mid).astype(BF16).astype(F32)
    slot = lax.broadcasted_iota(jnp.int32, c.shape, 1) % AUG_GROUP
    part = jnp.where(slot % AUG_PARTS == 0, hi, jnp.where(slot % AUG_PARTS == 1, mid, lo))
    used = slot < AUG_SLOTS
    qside = slot < AUG_PARTS
    qaug = jnp.where(used, jnp.where(qside, part, 1.0), 0.0)
    kaug = jnp.where(used, jnp.where(qside, 1.0, -part), 0.0)
    qaugt_ref[...] = qaug.T.astype(BF16)
    kaug_ref[...] = kaug.astype(BF16)


def _cum(flog3, fb):
    b, s, _ = flog3.shape
    return pl.pallas_call(
        _cum_body,
        grid=(b,),
        in_specs=[
            pl.BlockSpec((None, s, LANES), lambda i: (i, 0, 0)),
            pl.BlockSpec((1, LANES), lambda i: (0, 0)),
        ],
        out_specs=[
            pl.BlockSpec((None, LANES, s), lambda i: (i, 0, 0)),
            pl.BlockSpec((None, s, LANES), lambda i: (i, 0, 0)),
        ],
        out_shape=[
            jax.ShapeDtypeStruct((b, LANES, s), BF16),
            jax.ShapeDtypeStruct((b, s, LANES), BF16),
        ],
        compiler_params=pltpu.CompilerParams(dimension_semantics=("arbitrary",)),
        name="fox_cumsum",
    )(flog3, fb)


def _attend(q_tile, k_tile, vt_tile, n_far, near, sc_scr, p_scr, acc_scr):
    n_tiles = n_far + len(near)
    assert [j for j, _ in near] == list(range(n_far, n_tiles))
    maps = range(2)
    tq = sc_scr.shape[-1]

    def scores(t, slot):
        for w in maps:
            sc_scr[slot, w] = jnp.dot(k_tile(t, w), q_tile(w), preferred_element_type=F32)

    def values(t, slot, alphas):
        return tuple(alphas[w] * acc_scr[w] + jnp.dot(vt_tile(t, w), p_scr[slot, w], preferred_element_type=F32)
                     for w in maps)

    def iteration(t, slot, ms, alphas, bias, has_prev, has_next):
        if has_prev:
            accs = values(t - 1, 1 - slot, alphas)
            for w in maps:
                acc_scr[w] = accs[w]
        if has_next:
            scores(t + 1, 1 - slot)
        new_ms, new_alphas = [], []
        for w in maps:
            sc = sc_scr[slot, w] if bias is None else sc_scr[slot, w] + bias
            m_new = jnp.maximum(ms[w], jnp.max(sc, axis=0, keepdims=True))
            new_alphas.append(jnp.exp2(ms[w] - m_new))
            p_scr[slot, w] = jnp.exp2(sc - m_new).astype(BF16)
            new_ms.append(m_new)
        return tuple(new_ms), tuple(new_alphas)

    for w in maps:
        acc_scr[w] = jnp.zeros(acc_scr.shape[1:], F32)
    ms = (jnp.full((1, tq), NEG, F32),) * 2
    bias_of = lambda t: None if t < n_far else near[t - n_far][1]
    scores(0, 0)
    ms, alphas = iteration(0, 0, ms, None, bias_of(0), False, n_tiles > 1)
    t = 1
    n_pairs = max(n_far - 1, 0) // 2
    if n_pairs > 0:
        def pair(k, st):
            st = iteration(1 + 2 * k, 1, st[0], st[1], None, True, True)
            return iteration(2 + 2 * k, 0, st[0], st[1], None, True, True)
        ms, alphas = lax.fori_loop(0, n_pairs, pair, (ms, alphas))
        t = 1 + 2 * n_pairs
    while t < n_tiles:
        ms, alphas = iteration(t, t % 2, ms, alphas, bias_of(t), True, t + 1 < n_tiles)
        t += 1
    return values(n_tiles - 1, (n_tiles - 1) % 2, alphas)


def _diff_body(rel_ref, qt_ref, k_ref, vt_ref, lq1_ref, lk1_ref, lq2_ref, lk2_ref, sg_ref, o_ref,
               bias_scr, qa_scr, qb_scr, vx_scr, sc_scr, p_scr, acc_scr, *, lam_init):
    s = k_ref.shape[0]
    head = pl.program_id(0)

    @pl.when(pl.program_id(1) == 0)
    def _():
        ki = lax.broadcasted_iota(jnp.int32, (TK, TQ), 0)
        qi = lax.broadcasted_iota(jnp.int32, (TK, TQ), 1)
        far = rel_ref[N_BUCKETS - 1, head]
        for idx in range(3):
            dist = qi - ki + (1 - idx) * TK
            n = jnp.maximum(dist, 0)
            nf = jnp.maximum(n, 1).astype(F32)
            large = MAX_EXACT + (jnp.log(nf / MAX_EXACT) / math.log(MAX_DISTANCE / MAX_EXACT)
                                 * (N_BUCKETS - MAX_EXACT)).astype(jnp.int32)
            large = jnp.minimum(large, N_BUCKETS - 1)
            bucket = jnp.where(n < MAX_EXACT, n, large)
            bias = jnp.zeros((TK, TQ), F32)
            for bk in range(N_BUCKETS):
                bias = jnp.where(bucket == bk, rel_ref[bk, head], bias)
            bias_scr[idx] = jnp.where(dist >= 0, (bias - far) * LOG2E, NEG)

    rowi = lax.broadcasted_iota(jnp.int32, qt_ref.shape, 0)
    qt = qt_ref[...].astype(F32)
    qa_scr[...] = jnp.where(rowi < DIFF_HEAD_DIM, qt, 0.0).astype(BF16)
    qb_scr[...] = jnp.where(rowi < DIFF_HEAD_DIM, 0.0, qt).astype(BF16)
    vx_scr[0:LANES, :] = vt_ref[...]
    vx_scr[LANES:, :] = jnp.ones((BF16_ROWS, s), BF16)

    lam = (jnp.exp(jnp.sum(lq1_ref[...] * lk1_ref[...], axis=-1, keepdims=True))
           - jnp.exp(jnp.sum(lq2_ref[...] * lk2_ref[...], axis=-1, keepdims=True)) + lam_init)

    k_tile = lambda j, w: k_ref[pl.ds(j * TK, TK), :]
    vt_tile = lambda j, w: vx_scr[:, pl.ds(j * TK, TK)]
    for i in range(s // TQ):
        qsl = slice(i * TQ, (i + 1) * TQ)
        j0 = i * (TQ // TK)
        near = [(j0 - 1 + idx, bias_scr[idx]) for idx in range(3) if j0 - 1 + idx >= 0]
        n_far = max(j0 - 1, 0)
        q_tile = lambda w, qsl=qsl: (qa_scr, qb_scr)[w][:, qsl]
        acca, accb = _attend(q_tile, k_tile, vt_tile, n_far, near, sc_scr, p_scr, acc_scr)
        o = (acca[:LANES] / acca[LANES:LANES + 1] - lam * (accb[:LANES] / accb[LANES:LANES + 1]))
        o = o * lax.rsqrt(jnp.mean(o * o, axis=0, keepdims=True) + EPS)
        o_ref[qsl, :] = (o.T * sg_ref[...] * (1.0 - lam_init)).astype(o_ref.dtype)


def _diff_attn(main3, maint, rel_bias, lq1, lk1, lq2, lk2, subln_g, lam_init):
    b, s, _ = main3.shape
    small = lambda w: pl.BlockSpec((1, w), lambda h, i: (0, 0))
    trow = lambda base: pl.BlockSpec((None, LANES, s), lambda h, i: (i, base // LANES + h, 0))
    return pl.pallas_call(
        functools.partial(_diff_body, lam_init=lam_init),
        grid=(DIFF_HEADS, b),
        in_specs=[
            pl.BlockSpec(memory_space=pltpu.SMEM),
            trow(ROW_DQ),
            pl.BlockSpec((None, s, LANES), lambda h, i: (i, 0, COL_DK // LANES + h)),
            trow(ROW_DV),
            small(DIFF_HEAD_DIM), small(DIFF_HEAD_DIM), small(DIFF_HEAD_DIM), small(DIFF_HEAD_DIM),
            small(2 * DIFF_HEAD_DIM),
        ],
        out_specs=pl.BlockSpec((None, s, LANES), lambda h, i: (i, 0, h)),
        out_shape=jax.ShapeDtypeStruct((b, s, DIFF_WIDTH), BF16),
        scratch_shapes=[
            pltpu.VMEM((3, TK, TQ), F32),
            pltpu.VMEM((LANES, s), BF16),
            pltpu.VMEM((LANES, s), BF16),
            pltpu.VMEM((LANES + BF16_ROWS, s), BF16),
            pltpu.VMEM((2, 2, TK, TQ), F32),
            pltpu.VMEM((2, 2, TK, TQ), BF16),
            pltpu.VMEM((2, LANES + BF16_ROWS, TQ), F32),
        ],
        compiler_params=pltpu.CompilerParams(
            dimension_semantics=("arbitrary", "arbitrary"), vmem_limit_bytes=VMEM_LIMIT),
        name="diff_attn",
    )(rel_bias, maint, main3, maint, lq1, lk1, lq2, lk2, subln_g)


def _fox_body(qt_ref, k_ref, vt_ref, qaugt_ref, kaug_ref, o_ref,
              mask_scr, qa_scr, qb_scr, ka_scr, kb_scr, va_scr, vb_scr, sc_scr, p_scr, acc_scr):
    s = k_ref.shape[0]
    pair = pl.program_id(0)

    @pl.when(pl.program_id(1) == 0)
    def _():
        ki = lax.broadcasted_iota(jnp.int32, (TK, TQ), 0)
        qi = lax.broadcasted_iota(jnp.int32, (TK, TQ), 1)
        for idx in range(TQ // TK):
            mask_scr[idx] = jnp.where(qi >= ki + idx * TK, 0.0, NEG)

    half = FOX_HEAD_DIM
    lo_a, lo_b = half + AUG_GROUP * pair, AUG_GROUP * pair

    def split(x, aug, idx):
        is_a = idx < half
        aug_a = jnp.where((idx >= lo_a) & (idx < lo_a + AUG_SLOTS), aug, 0.0)
        aug_b = jnp.where((idx >= lo_b) & (idx < lo_b + AUG_SLOTS), aug, 0.0)
        return jnp.where(is_a, x, aug_a).astype(BF16), jnp.where(is_a, aug_b, x).astype(BF16)

    rowi = lax.broadcasted_iota(jnp.int32, qt_ref.shape, 0)
    lanei = lax.broadcasted_iota(jnp.int32, k_ref.shape, 1)
    qa_scr[...], qb_scr[...] = split(qt_ref[...].astype(F32), qaugt_ref[...].astype(F32), rowi)
    ka_scr[...], kb_scr[...] = split(k_ref[...].astype(F32), kaug_ref[...].astype(F32), lanei)
    vt = vt_ref[...].astype(F32)
    va_scr[...] = jnp.where(rowi < half, vt, 1.0).astype(BF16)
    vb_scr[...] = jnp.where(rowi < half, 1.0, vt).astype(BF16)

    k_tile = lambda j, w: (ka_scr, kb_scr)[w][pl.ds(j * TK, TK), :]
    vt_tile = lambda j, w: (va_scr, vb_scr)[w][:, pl.ds(j * TK, TK)]
    rows = lax.broadcasted_iota(jnp.int32, (LANES, TQ), 0)
    for i in range(s // TQ):
        qsl = slice(i * TQ, (i + 1) * TQ)
        j0 = i * (TQ // TK)
        near = [(j0 + idx, mask_scr[idx]) for idx in range(TQ // TK)]
        q_tile = lambda w, qsl=qsl: (qa_scr, qb_scr)[w][:, qsl]
        acca, accb = _attend(q_tile, k_tile, vt_tile, j0, near, sc_scr, p_scr, acc_scr)
        o = jnp.where(rows < half, acca / acca[half:half + 1], accb / accb[0:1])
        o_ref[qsl, :] = o.T.astype(o_ref.dtype)


def _fox_attn(main3, maint, qaugt, kaug):
    b, s, _ = main3.shape
    trow = lambda base: pl.BlockSpec((None, LANES, s), lambda p, i: (i, base // LANES + p, 0))
    tsc = pltpu.VMEM((LANES, s), BF16)
    return pl.pallas_call(
        _fox_body,
        grid=(FOX_HEADS // 2, b),
        in_specs=[
            trow(ROW_FQ),
            pl.BlockSpec((None, s, LANES), lambda p, i: (i, 0, COL_FK // LANES + p)),
            trow(ROW_FV),
            pl.BlockSpec((None, LANES, s), lambda p, i: (i, 0, 0)),
            pl.BlockSpec((None, s, LANES), lambda p, i: (i, 0, 0)),
        ],
        out_specs=pl.BlockSpec((None, s, LANES), lambda p, i: (i, 0, p)),
        out_shape=jax.ShapeDtypeStruct((b, s, FOX_WIDTH), BF16),
        scratch_shapes=[
            pltpu.VMEM((TQ // TK, TK, TQ), F32),
            tsc, tsc, pltpu.VMEM((s, LANES), BF16), pltpu.VMEM((s, LANES), BF16), tsc, tsc,
            pltpu.VMEM((2, 2, TK, TQ), F32),
            pltpu.VMEM((2, 2, TK, TQ), BF16),
            pltpu.VMEM((2, LANES, TQ), F32),
        ],
        compiler_params=pltpu.CompilerParams(
            dimension_semantics=("arbitrary", "arbitrary"), vmem_limit_bytes=VMEM_LIMIT),
        name="fox_attn",
    )(maint, main3, maint, qaugt, kaug)


def _merge_body(x_ref, g0_ref, g1_ref, g2_ref, gb_ref, yr_ref, yd_ref, yf_ref, wr_ref, wd_ref, wf_ref, wo_ref,
                o_ref):
    def branch(g_ref, k, y_ref, w_ref):
        gate = jax.nn.sigmoid(g_ref[...].astype(F32) + gb_ref[k:k + 1, :])
        return gate * jnp.dot(y_ref[...], w_ref[...], preferred_element_type=F32)

    m = branch(g0_ref, 0, yr_ref, wr_ref) + branch(g1_ref, 1, yd_ref, wd_ref) + branch(g2_ref, 2, yf_ref, wf_ref)
    o_ref[...] = x_ref[...] + jnp.dot(m.astype(BF16), wo_ref[...], preferred_element_type=F32)


def _merge(x2, main2, gate_b, y_rnn, y_diff, y_fox, w_r, w_d, w_f, w_o):
    t, d = x2.shape
    tm = min(TM_MERGE, t)
    rows = lambda w, c=0: pl.BlockSpec((tm, w), lambda i: (i, c))
    whole = lambda a: pl.BlockSpec(a.shape, lambda i: (0, 0))
    g0 = COL_GATES // d
    return pl.pallas_call(
        _merge_body,
        grid=(t // tm,),
        in_specs=[
            rows(d), rows(d, g0), rows(d, g0 + 1), rows(d, g0 + 2), whole(gate_b),
            rows(D_RNN), rows(DIFF_WIDTH), rows(FOX_WIDTH),
            whole(w_r), whole(w_d), whole(w_f), whole(w_o),
        ],
        out_specs=rows(d),
        out_shape=jax.ShapeDtypeStruct((t, d), F32),
        compiler_params=pltpu.CompilerParams(dimension_semantics=("arbitrary",), vmem_limit_bytes=VMEM_LIMIT),
        name="merge",
    )(x2, main2, main2, main2, gate_b, y_rnn, y_diff, y_fox, w_r, w_d, w_f, w_o)


def _ffn_body(x_ref, g_ref, wu_ref, cw_ref, cb_ref, wd_ref, gout_ref, o_ref, h_scr, carry_scr, u_scr, act_scr,
              *, tiles_per_seq, out_norm):
    tm = x_ref.shape[0]
    nsub = D_FF // FF_SUB
    h_scr[...] = _rms(x_ref[...], g_ref[...]).astype(BF16)
    o_ref[...] = x_ref[...]

    @pl.when((pl.program_id(0) % tiles_per_seq) == 0)
    def _():
        carry_scr[...] = jnp.zeros(carry_scr.shape, F32)

    row = lax.broadcasted_iota(jnp.int32, (FF_ROWS, FF_SUB), 0)
    col_of = lambda c: (pl.ds(c * FF_SUB, FF_SUB), pl.ds(D_FF + c * FF_SUB, FF_SUB))

    def up(c, slot):
        for k, cols in enumerate(col_of(c)):
            u_scr[slot, k] = jnp.dot(h_scr[...], wu_ref[:, cols], preferred_element_type=F32)

    def down(c, slot):
        o_ref[...] += jnp.dot(act_scr[slot], wd_ref[pl.ds(c * FF_SUB, FF_SUB), :], preferred_element_type=F32)

    def activate(c, slot):
        for r in range(tm // FF_ROWS):
            rows = pl.ds(r * FF_ROWS, FF_ROWS)

            def conv(k, cols):
                u = u_scr[slot, k, rows, :]
                prev = (carry_scr[:, cols] if r == 0 else
                        u_scr[slot, k, pl.ds(r * FF_ROWS - SUBLANES, SUBLANES), :])
                u1 = jnp.where(row == 0, prev[SUBLANES - 1:, :], pltpu.roll(u, 1, 0))
                u2 = jnp.where(row == 0, prev[SUBLANES - 2:SUBLANES - 1, :],
                               jnp.where(row == 1, prev[SUBLANES - 1:, :], pltpu.roll(u, 2, 0)))
                return cb_ref[:, cols] + u2 * cw_ref[0:1, cols] + u1 * cw_ref[1:2, cols] + u * cw_ref[2:3, cols]

            gate, val = (conv(k, cols) for k, cols in enumerate(col_of(c)))
            act_scr[slot, rows, :] = (jax.nn.gelu(gate) * val).astype(BF16)
        for k, cols in enumerate(col_of(c)):
            carry_scr[:, cols] = u_scr[slot, k, pl.ds(tm - SUBLANES, SUBLANES), :]

    up(0, 0)
    for c in range(nsub):
        if c + 1 < nsub:
            up(c + 1, (c + 1) % 2)
        activate(c, c % 2)
        if c > 0:
            down(c - 1, (c - 1) % 2)
    down(nsub - 1, (nsub - 1) % 2)
    if out_norm:
        o_ref[...] = _rms(o_ref[...], gout_ref[...])


def _ffn(x2, g, w_up, conv_w, conv_b, w_down, g_out, seq_len, out_norm):
    t, d = x2.shape
    tm = min(TM_FFN, seq_len)
    return pl.pallas_call(
        functools.partial(_ffn_body, tiles_per_seq=seq_len // tm, out_norm=out_norm),
        grid=(t // tm,),
        in_specs=[
            pl.BlockSpec((tm, d), lambda i: (i, 0)),
            _resident(g.shape), _resident(w_up.shape), _resident(conv_w.shape), _resident(conv_b.shape),
            _resident(w_down.shape), _resident(g_out.shape),
        ],
        out_specs=pl.BlockSpec((tm, d), lambda i: (i, 0)),
        out_shape=jax.ShapeDtypeStruct((t, d), F32),
        scratch_shapes=[pltpu.VMEM((tm, d), BF16), pltpu.VMEM((SUBLANES, 2 * D_FF), F32),
                        pltpu.VMEM((2, 2, tm, FF_SUB), F32), pltpu.VMEM((2, tm, FF_SUB), BF16)],
        compiler_params=pltpu.CompilerParams(dimension_semantics=("arbitrary",), vmem_limit_bytes=VMEM_LIMIT),
        name="ffn",
    )(x2, g, w_up, conv_w, conv_b, w_down, g_out)


def _aug_head_of_lane():
    heads = []
    for lane in range(LANES):
        group, slot = (lane % (LANES // 2)) // AUG_GROUP, lane % AUG_GROUP
        if group < FOX_HEADS // 2 and slot < AUG_SLOTS:
            heads.append(2 * group + (1 if lane < LANES // 2 else 0))
        else:
            heads.append(-1)
    return heads


def _prep_w_in(w_in_l, fox_b_f_l):
    seg = lambda a, n: w_in_l[:, a:a + n]
    w_row = jnp.concatenate([seg(W_XRNN, D_RNN), seg(W_GRNN, D_RNN), seg(W_DK, DIFF_QK), seg(W_FK, FOX_WIDTH),
                             seg(W_GATES, N_BRANCH * D_MODEL)], axis=1)
    w_t = jnp.concatenate([seg(W_DQ, DIFF_QK) * (DIFF_HEAD_DIM ** -0.5 * LOG2E),
                           seg(W_FQ, FOX_WIDTH) * (FOX_HEAD_DIM ** -0.5 * LOG2E),
                           seg(W_DV, DIFF_WIDTH), seg(W_FV, FOX_WIDTH)], axis=1).T
    heads = _aug_head_of_lane()
    idx = jnp.array([max(h, 0) for h in heads], jnp.int32)
    live = jnp.array([1.0 if h >= 0 else 0.0 for h in heads], F32)
    w_f = seg(W_FLOG, FOX_HEADS)[:, idx] * live
    fb = (fox_b_f_l[idx] * live).reshape(1, LANES)
    return w_row.astype(BF16), w_t.astype(BF16), w_f.astype(BF16), fb


def kernel(x, norm1_g, w_in, rnn_conv_w, rnn_conv_b, rg_w_r, rg_b_r, rg_w_i, rg_b_i, rg_a,
           diff_lq1, diff_lk1, diff_lq2, diff_lk2, diff_subln_g, rel_bias, fox_b_f, gate_b,
           w_br_rnn, w_br_diff, w_br_fox, w_out, norm2_g, ffn_up, ffn_conv_w, ffn_conv_b,
           ffn_down, final_g):
    bsz, s_len, d = x.shape
    t = bsz * s_len
    x2 = x.reshape(t, d)
    row = lambda v: v.reshape(1, -1)
    for l in range(DEPTH):
        w_row, w_t, w_f, fb = _prep_w_in(w_in[l], fox_b_f[l])
        main2, maint, flog = _inproj(x2, row(norm1_g[l]), w_row, w_t, w_f, bsz, s_len)
        main3 = main2.reshape(bsz, s_len, N_ROWOUT)

        y_rnn = _rnn(main3, rnn_conv_w[l], row(rnn_conv_b[l]), rg_w_r[l].astype(BF16), row(rg_b_r[l]),
                     rg_w_i[l].astype(BF16), row(rg_b_i[l]), row(rg_a[l]))

        lam_init = 0.8 - 0.6 * math.exp(-0.3 * l)
        y_diff = _diff_attn(main3, maint, rel_bias, row(diff_lq1[l]), row(diff_lk1[l]), row(diff_lq2[l]),
                            row(diff_lk2[l]), row(diff_subln_g[l]), lam_init)

        qaugt, kaug = _cum(flog.reshape(bsz, s_len, LANES), fb)
        y_fox = _fox_attn(main3, maint, qaugt, kaug)

        x2 = _merge(x2, main2, gate_b[l], y_rnn.reshape(t, D_RNN), y_diff.reshape(t, DIFF_WIDTH),
                    y_fox.reshape(t, FOX_WIDTH), w_br_rnn[l].astype(BF16), w_br_diff[l].astype(BF16),
                    w_br_fox[l].astype(BF16), w_out[l].astype(BF16))

        x2 = _ffn(x2, row(norm2_g[l]), ffn_up[l].astype(BF16), ffn_conv_w[l], row(ffn_conv_b[l]),
                  ffn_down[l].astype(BF16), row(final_g), s_len, out_norm=(l == DEPTH - 1))
    return x2.reshape(bsz, s_len, d)
```

```python
import functools
import math

import jax
import jax.numpy as jnp
from jax import lax
from jax.experimental import pallas as pl
from jax.experimental.pallas import tpu as pltpu

F32 = jnp.float32
BF16 = jnp.bfloat16

D_MODEL = 1024
DEPTH = 2
D_RNN = D_MODEL
RNN_BLOCKS = 8
RNN_BLOCK_W = D_RNN // RNN_BLOCKS
RG_LRU_C = 8.0
DIFF_HEADS = 4
DIFF_HEAD_DIM = 64
DIFF_QK = DIFF_HEADS * 2 * DIFF_HEAD_DIM
DIFF_WIDTH = DIFF_HEADS * 2 * DIFF_HEAD_DIM
FOX_HEADS = 8
FOX_HEAD_DIM = 64
FOX_WIDTH = FOX_HEADS * FOX_HEAD_DIM
N_BUCKETS = 32
MAX_EXACT = N_BUCKETS // 2
MAX_DISTANCE = 128
D_FF = ((8 * D_MODEL // 3 + 127) // 128) * 128
N_BRANCH = 3
EPS = 1e-6
LOG2E = 1.4426950408889634
NEG = -1e30

LANES = 128
SUBLANES = 8
BF16_ROWS = 16

W_XRNN = 0
W_GRNN = W_XRNN + D_RNN
W_DQ = W_GRNN + D_RNN
W_DK = W_DQ + DIFF_QK
W_DV = W_DK + DIFF_QK
W_FQ = W_DV + DIFF_WIDTH
W_FK = W_FQ + FOX_WIDTH
W_FV = W_FK + FOX_WIDTH
W_FLOG = W_FV + FOX_WIDTH
W_GATES = W_FLOG + FOX_HEADS

COL_XRNN = 0
COL_GRNN = COL_XRNN + D_RNN
COL_DK = COL_GRNN + D_RNN
COL_FK = COL_DK + DIFF_QK
COL_GATES = COL_FK + FOX_WIDTH
N_ROWOUT = COL_GATES + N_BRANCH * D_MODEL
ROW_DQ = 0
ROW_FQ = ROW_DQ + DIFF_QK
ROW_DV = ROW_FQ + FOX_WIDTH
ROW_FV = ROW_DV + DIFF_WIDTH
N_TOUT = ROW_FV + FOX_WIDTH

TQ = 512
TK = 256
TM_PROJ = 512
TN_PROJ = 1024
TM_MERGE = 512
TM_FFN = 512
FF_CHUNK = D_FF // 2
RNN_HALF = D_RNN // 2
VMEM_LIMIT = 56 * 1024 * 1024

AUG_GROUP = 8
AUG_SLOTS = 6
AUG_PARTS = 3


def _softplus(z):
    return jnp.maximum(z, 0.0) + jnp.log1p(jnp.exp(-jnp.abs(z)))


def _rms(x, g):
    return x * lax.rsqrt(jnp.mean(x * x, axis=-1, keepdims=True) + EPS) * g


def _sigmoid(z):
    return 0.5 * jnp.tanh(0.5 * z) + 0.5


def _shift_rows(u, before, k):
    rows, width = u.shape
    groups = rows // SUBLANES
    rot = pltpu.roll(u.reshape(groups, SUBLANES, width), k, 1)
    prev = jnp.concatenate([pltpu.roll(before, k, 0)[None], rot[:-1]], axis=0)
    sub = lax.broadcasted_iota(jnp.int32, rot.shape, 1)
    return jnp.where(sub < k, prev, rot).reshape(rows, width)


def _inproj_body(x_ref, g_ref, w_ref, wt_ref, wf_ref, out_ref, outt_ref, flog_ref, h_scr):
    h_scr[...] = _rms(x_ref[...], g_ref[...]).astype(BF16)
    flog_ref[...] = jnp.dot(h_scr[...], wf_ref[...], preferred_element_type=F32)
    for j in range(w_ref.shape[1] // TN_PROJ):
        cols = slice(j * TN_PROJ, (j + 1) * TN_PROJ)
        out_ref[:, cols] = jnp.dot(h_scr[...], w_ref[:, cols], preferred_element_type=F32).astype(out_ref.dtype)
    for j in range(wt_ref.shape[0] // TN_PROJ):
        rows = slice(j * TN_PROJ, (j + 1) * TN_PROJ)
        outt_ref[rows, :] = lax.dot_general(wt_ref[rows, :], h_scr[...], (((1,), (1,)), ((), ())),
                                            preferred_element_type=F32).astype(outt_ref.dtype)


def _resident(shape):
    return pl.BlockSpec(shape, lambda *_: (0,) * len(shape), pipeline_mode=pl.Buffered(1))


def _inproj(x2, g, w_row, w_t, w_f, bsz, s_len):
    t, d = x2.shape
    tm = min(TM_PROJ, s_len)
    tps = s_len // tm
    return pl.pallas_call(
        _inproj_body,
        grid=(t // tm,),
        in_specs=[
            pl.BlockSpec((tm, d), lambda i: (i, 0)),
            _resident(g.shape), _resident(w_row.shape), _resident(w_t.shape), _resident(w_f.shape),
        ],
        out_specs=[
            pl.BlockSpec((tm, N_ROWOUT), lambda i: (i, 0)),
            pl.BlockSpec((None, N_TOUT, tm), lambda i: (i // tps, 0, i % tps)),
            pl.BlockSpec((tm, LANES), lambda i: (i, 0)),
        ],
        out_shape=[
            jax.ShapeDtypeStruct((t, N_ROWOUT), BF16),
            jax.ShapeDtypeStruct((bsz, N_TOUT, s_len), BF16),
            jax.ShapeDtypeStruct((t, LANES), F32),
        ],
        scratch_shapes=[pltpu.VMEM((tm, d), BF16)],
        compiler_params=pltpu.CompilerParams(dimension_semantics=("arbitrary",), vmem_limit_bytes=VMEM_LIMIT),
        name="inproj",
    )(x2, g, w_row, w_t, w_f)


def _rnn_body(xr_ref, gr_ref, cw_ref, cb_ref, wr_ref, br_ref, wi_ref, bi_ref, ap_ref, y_ref, a_scr, u_scr):
    s, width = xr_ref.shape
    nblk = width // RNN_BLOCK_W
    half = pl.program_id(1)
    zeros = jnp.zeros((SUBLANES, RNN_BLOCK_W), F32)
    for n in range(nblk):
        sl = slice(n * RNN_BLOCK_W, (n + 1) * RNN_BLOCK_W)
        x = xr_ref[:, sl].astype(F32)
        xc = (cb_ref[:, sl] + _shift_rows(x, zeros, 3) * cw_ref[0:1, sl] + _shift_rows(x, zeros, 2) * cw_ref[1:2, sl]
              + _shift_rows(x, zeros, 1) * cw_ref[2:3, sl] + x * cw_ref[3:4, sl])
        xb = xc.astype(BF16)
        blk = half * nblk + n
        zr = jnp.dot(xb, wr_ref[blk], preferred_element_type=F32) + br_ref[:, sl]
        gi = _sigmoid(jnp.dot(xb, wi_ref[blk], preferred_element_type=F32) + bi_ref[:, sl])
        k = (-0.5 * RG_LRU_C) * _softplus(-ap_ref[:, sl])
        a = jnp.exp(k * jnp.tanh(0.5 * zr) + k)
        a_scr[:, sl] = a
        y = 1.0 - a * a
        u_scr[:, sl] = jnp.where(y > 0.0, y * lax.rsqrt(y), 0.0) * (gi * xc)

    row8 = lax.broadcasted_iota(jnp.int32, (SUBLANES, width), 0)

    def group(k, hprev):
        off = pl.multiple_of(k * SUBLANES, SUBLANES)
        a = a_scr[pl.ds(off, SUBLANES), :]
        u = u_scr[pl.ds(off, SUBLANES), :]
        for d in (1, 2, 4):
            a_sh = jnp.where(row8 >= d, pltpu.roll(a, d, 0), 1.0)
            u_sh = jnp.where(row8 >= d, pltpu.roll(u, d, 0), 0.0)
            u = a * u_sh + u
            a = a * a_sh
        h = a * hprev + u
        u_scr[pl.ds(off, SUBLANES), :] = h
        return h[SUBLANES - 1:SUBLANES, :]

    lax.fori_loop(0, s // SUBLANES, group, jnp.zeros((1, width), F32))

    for n in range(nblk):
        sl = slice(n * RNN_BLOCK_W, (n + 1) * RNN_BLOCK_W)
        y_ref[:, sl] = (jax.nn.gelu(gr_ref[:, sl].astype(F32)) * u_scr[:, sl]).astype(y_ref.dtype)


def _rnn(main3, conv_w, conv_b, w_r, b_r, w_i, b_i, a_param):
    b, s, _ = main3.shape
    w = RNN_HALF
    nh = D_RNN // w
    vec = lambda k: pl.BlockSpec((k, w), lambda i, j: (0, j))
    full3 = pl.BlockSpec((RNN_BLOCKS, RNN_BLOCK_W, RNN_BLOCK_W), lambda i, j: (0, 0, 0))
    return pl.pallas_call(
        _rnn_body,
        grid=(b, nh),
        in_specs=[
            pl.BlockSpec((None, s, w), lambda i, j: (i, 0, COL_XRNN // w + j)),
            pl.BlockSpec((None, s, w), lambda i, j: (i, 0, COL_GRNN // w + j)),
            vec(4), vec(1), full3, vec(1), full3, vec(1), vec(1),
        ],
        out_specs=pl.BlockSpec((None, s, w), lambda i, j: (i, 0, j)),
        out_shape=jax.ShapeDtypeStruct((b, s, D_RNN), BF16),
        scratch_shapes=[pltpu.VMEM((s, w), F32), pltpu.VMEM((s, w), F32)],
        compiler_params=pltpu.CompilerParams(
            dimension_semantics=("arbitrary", "arbitrary"), vmem_limit_bytes=VMEM_LIMIT),
        name="rnn",
    )(main3, main3, conv_w, conv_b, w_r, b_r, w_i, b_i, a_param)


def _cum_body(flog_ref, fb_ref, qaugt_ref, kaug_ref):
    s = flog_ref.shape[0]
    z = flog_ref[...] + fb_ref[...]
    c = -_softplus(-z) * LOG2E
    row = lax.broadcasted_iota(jnp.int32, c.shape, 0)
    d = 1
    while d < s:
        c = c + jnp.where(row >= d, pltpu.roll(c, d, 0), 0.0)
        d *= 2
    hi = c.astype(BF16).astype(F32)
    mid = (c - hi).astype(BF16).astype(F32)
    lo = (c - hi - mid).astype(BF16).astype(F32)
    slot = lax.broadcasted_iota(jnp.int32, c.shape, 1) % AUG_GROUP
    part = jnp.where(slot % AUG_PARTS == 0, hi, jnp.where(slot % AUG_PARTS == 1, mid, lo))
    used = slot < AUG_SLOTS
    qside = slot < AUG_PARTS
    qaug = jnp.where(used, jnp.where(qside, part, 1.0), 0.0)
    kaug = jnp.where(used, jnp.where(qside, 1.0, -part), 0.0)
    qaugt_ref[...] = qaug.T.astype(BF16)
    kaug_ref[...] = kaug.astype(BF16)


def _cum(flog3, fb):
    b, s, _ = flog3.shape
    return pl.pallas_call(
        _cum_body,
        grid=(b,),
        in_specs=[
            pl.BlockSpec((None, s, LANES), lambda i: (i, 0, 0)),
            pl.BlockSpec((1, LANES), lambda i: (0, 0)),
        ],
        out_specs=[
            pl.BlockSpec((None, LANES, s), lambda i: (i, 0, 0)),
            pl.BlockSpec((None, s, LANES), lambda i: (i, 0, 0)),
        ],
        out_shape=[
            jax.ShapeDtypeStruct((b, LANES, s), BF16),
            jax.ShapeDtypeStruct((b, s, LANES), BF16),
        ],
        compiler_params=pltpu.CompilerParams(dimension_semantics=("arbitrary",)),
        name="fox_cumsum",
    )(flog3, fb)


def _attend(q_tile, k_tile, vt_tile, n_far, near, sc_scr, p_scr, acc_scr):
    n_tiles = n_far + len(near)
    assert [j for j, _, _ in near] == list(range(n_far, n_tiles))
    maps = range(2)
    tq = sc_scr.shape[-1]
    FAR = (None, 0)
    kind = lambda t: FAR if t < n_far else near[t - n_far][1:]

    def scores(t, slot, tile):
        bias, lo = tile
        tops = []
        for w in maps:
            sc = jnp.dot(k_tile(t, w), q_tile(w)[:, lo:], preferred_element_type=F32)
            if bias is not None:
                sc = sc + bias[:, lo:]
            sc_scr[slot, w, :, lo:] = sc
            tops.append(jnp.max(sc, axis=0, keepdims=True))
        return tuple(tops)

    def values(t, slot, alphas, tile):
        lo = tile[1]
        for w in maps:
            acc_scr[w, :, lo:] = alphas[w] * acc_scr[w, :, lo:] + jnp.dot(
                vt_tile(t, w), p_scr[slot, w, :, lo:], preferred_element_type=F32)

    def iteration(t, slot, ms, alphas, tops, prev, cur, nxt):
        if prev is not None:
            values(t - 1, 1 - slot, alphas, prev)
        next_tops = scores(t + 1, 1 - slot, nxt) if nxt is not None else tops
        lo = cur[1]
        new_ms, new_alphas = [], []
        for w in maps:
            m_old = ms[w][:, lo:]
            m_new = jnp.maximum(m_old, tops[w])
            new_alphas.append(jnp.exp2(m_old - m_new))
            p_scr[slot, w, :, lo:] = jnp.exp2(sc_scr[slot, w, :, lo:] - m_new).astype(BF16)
            new_ms.append(m_new if lo == 0 else jnp.concatenate([ms[w][:, :lo], m_new], axis=1))
        return tuple(new_ms), tuple(new_alphas), next_tops

    for w in maps:
        acc_scr[w] = jnp.zeros(acc_scr.shape[1:], F32)
    ms = (jnp.full((1, tq), NEG, F32),) * 2
    tops = scores(0, 0, kind(0))
    ms, alphas, tops = iteration(0, 0, ms, None, tops, None, kind(0), kind(1) if n_tiles > 1 else None)
    t = 1
    n_pairs = max(n_far - 2, 0) // 2
    if n_pairs > 0:
        def pair(k, st):
            st = iteration(1 + 2 * k, 1, *st, FAR, FAR, FAR)
            return iteration(2 + 2 * k, 0, *st, FAR, FAR, FAR)
        ms, alphas, tops = lax.fori_loop(0, n_pairs, pair, (ms, alphas, tops))
        t = 1 + 2 * n_pairs
    while t < n_tiles:
        ms, alphas, tops = iteration(t, t % 2, ms, alphas, tops, kind(t - 1), kind(t),
                                     kind(t + 1) if t + 1 < n_tiles else None)
        t += 1
    values(n_tiles - 1, (n_tiles - 1) % 2, alphas, kind(n_tiles - 1))
    return acc_scr[0], acc_scr[1]


def _diff_body(rel_ref, qt_ref, k_ref, vt_ref, lq1_ref, lk1_ref, lq2_ref, lk2_ref, sg_ref, o_ref,
               bias_scr, qa_scr, qb_scr, vx_scr, sc_scr, p_scr, acc_scr, *, lam_init):
    s = k_ref.shape[0]
    head = pl.program_id(0)

    @pl.when(pl.program_id(1) == 0)
    def _():
        ki = lax.broadcasted_iota(jnp.int32, (TK, TQ), 0)
        qi = lax.broadcasted_iota(jnp.int32, (TK, TQ), 1)
        far = rel_ref[N_BUCKETS - 1, head]
        for idx in range(3):
            dist = qi - ki + (1 - idx) * TK
            n = jnp.maximum(dist, 0)
            nf = jnp.maximum(n, 1).astype(F32)
            large = MAX_EXACT + (jnp.log(nf / MAX_EXACT) / math.log(MAX_DISTANCE / MAX_EXACT)
                                 * (N_BUCKETS - MAX_EXACT)).astype(jnp.int32)
            large = jnp.minimum(large, N_BUCKETS - 1)
            bucket = jnp.where(n < MAX_EXACT, n, large)
            bias = jnp.zeros((TK, TQ), F32)
            for bk in range(N_BUCKETS):
                bias = jnp.where(bucket == bk, rel_ref[bk, head], bias)
            bias_scr[idx] = jnp.where(dist >= 0, (bias - far) * LOG2E, NEG)

    rowi = lax.broadcasted_iota(jnp.int32, qt_ref.shape, 0)
    qt = qt_ref[...].astype(F32)
    qa_scr[...] = jnp.where(rowi < DIFF_HEAD_DIM, qt, 0.0).astype(BF16)
    qb_scr[...] = jnp.where(rowi < DIFF_HEAD_DIM, 0.0, qt).astype(BF16)
    vx_scr[0:LANES, :] = vt_ref[...]
    vx_scr[LANES:, :] = jnp.ones((BF16_ROWS, s), BF16)

    lam = (jnp.exp(jnp.sum(lq1_ref[...] * lk1_ref[...], axis=-1, keepdims=True))
           - jnp.exp(jnp.sum(lq2_ref[...] * lk2_ref[...], axis=-1, keepdims=True)) + lam_init)

    k_tile = lambda j, w: k_ref[pl.ds(j * TK, TK), :]
    vt_tile = lambda j, w: vx_scr[:, pl.ds(j * TK, TK)]
    for i in range(s // TQ):
        qsl = slice(i * TQ, (i + 1) * TQ)
        j0 = i * (TQ // TK)
        near = [(j0 - 1 + idx, bias_scr[idx], max(idx - 1, 0) * TK) for idx in range(3) if j0 - 1 + idx >= 0]
        n_far = max(j0 - 1, 0)
        q_tile = lambda w, qsl=qsl: (qa_scr, qb_scr)[w][:, qsl]
        acca, accb = _attend(q_tile, k_tile, vt_tile, n_far, near, sc_scr, p_scr, acc_scr)
        o = (acca[:LANES] / acca[LANES:LANES + 1] - lam * (accb[:LANES] / accb[LANES:LANES + 1]))
        o = o * lax.rsqrt(jnp.mean(o * o, axis=0, keepdims=True) + EPS)
        o_ref[qsl, :] = (o.T * sg_ref[...] * (1.0 - lam_init)).astype(o_ref.dtype)


def _diff_attn(main3, maint, rel_bias, lq1, lk1, lq2, lk2, subln_g, lam_init):
    b, s, _ = main3.shape
    small = lambda w: pl.BlockSpec((1, w), lambda h, i: (0, 0))
    trow = lambda base: pl.BlockSpec((None, LANES, s), lambda h, i: (i, base // LANES + h, 0))
    return pl.pallas_call(
        functools.partial(_diff_body, lam_init=lam_init),
        grid=(DIFF_HEADS, b),
        in_specs=[
            pl.BlockSpec(memory_space=pltpu.SMEM),
            trow(ROW_DQ),
            pl.BlockSpec((None, s, LANES), lambda h, i: (i, 0, COL_DK // LANES + h)),
            trow(ROW_DV),
            small(DIFF_HEAD_DIM), small(DIFF_HEAD_DIM), small(DIFF_HEAD_DIM), small(DIFF_HEAD_DIM),
            small(2 * DIFF_HEAD_DIM),
        ],
        out_specs=pl.BlockSpec((None, s, LANES), lambda h, i: (i, 0, h)),
        out_shape=jax.ShapeDtypeStruct((b, s, DIFF_WIDTH), BF16),
        scratch_shapes=[
            pltpu.VMEM((3, TK, TQ), F32),
            pltpu.VMEM((LANES, s), BF16),
            pltpu.VMEM((LANES, s), BF16),
            pltpu.VMEM((LANES + BF16_ROWS, s), BF16),
            pltpu.VMEM((2, 2, TK, TQ), F32),
            pltpu.VMEM((2, 2, TK, TQ), BF16),
            pltpu.VMEM((2, LANES + BF16_ROWS, TQ), F32),
        ],
        compiler_params=pltpu.CompilerParams(
            dimension_semantics=("arbitrary", "arbitrary"), vmem_limit_bytes=VMEM_LIMIT),
        name="diff_attn",
    )(rel_bias, maint, main3, maint, lq1, lk1, lq2, lk2, subln_g)


def _fox_body(qt_ref, k_ref, vt_ref, qaugt_ref, kaug_ref, o_ref,
              mask_scr, qa_scr, qb_scr, ka_scr, kb_scr, va_scr, vb_scr, sc_scr, p_scr, acc_scr):
    s = k_ref.shape[0]
    pair = pl.program_id(0)

    @pl.when(pl.program_id(1) == 0)
    def _():
        ki = lax.broadcasted_iota(jnp.int32, (TK, TQ), 0)
        qi = lax.broadcasted_iota(jnp.int32, (TK, TQ), 1)
        for idx in range(TQ // TK):
            mask_scr[idx] = jnp.where(qi >= ki + idx * TK, 0.0, NEG)

    half = FOX_HEAD_DIM
    lo_a, lo_b = half + AUG_GROUP * pair, AUG_GROUP * pair

    def split(x, aug, idx):
        is_a = idx < half
        aug_a = jnp.where((idx >= lo_a) & (idx < lo_a + AUG_SLOTS), aug, 0.0)
        aug_b = jnp.where((idx >= lo_b) & (idx < lo_b + AUG_SLOTS), aug, 0.0)
        return jnp.where(is_a, x, aug_a).astype(BF16), jnp.where(is_a, aug_b, x).astype(BF16)

    rowi = lax.broadcasted_iota(jnp.int32, qt_ref.shape, 0)
    lanei = lax.broadcasted_iota(jnp.int32, k_ref.shape, 1)
    qa_scr[...], qb_scr[...] = split(qt_ref[...].astype(F32), qaugt_ref[...].astype(F32), rowi)
    ka_scr[...], kb_scr[...] = split(k_ref[...].astype(F32), kaug_ref[...].astype(F32), lanei)
    vt = vt_ref[...].astype(F32)
    va_scr[...] = jnp.where(rowi < half, vt, 1.0).astype(BF16)
    vb_scr[...] = jnp.where(rowi < half, 1.0, vt).astype(BF16)

    k_tile = lambda j, w: (ka_scr, kb_scr)[w][pl.ds(j * TK, TK), :]
    vt_tile = lambda j, w: (va_scr, vb_scr)[w][:, pl.ds(j * TK, TK)]
    rows = lax.broadcasted_iota(jnp.int32, (LANES, TQ), 0)
    for i in range(s // TQ):
        qsl = slice(i * TQ, (i + 1) * TQ)
        j0 = i * (TQ // TK)
        near = [(j0 + idx, mask_scr[idx], idx * TK) for idx in range(TQ // TK)]
        q_tile = lambda w, qsl=qsl: (qa_scr, qb_scr)[w][:, qsl]
        acca, accb = _attend(q_tile, k_tile, vt_tile, j0, near, sc_scr, p_scr, acc_scr)
        o = jnp.where(rows < half, acca / acca[half:half + 1], accb / accb[0:1])
        o_ref[qsl, :] = o.T.astype(o_ref.dtype)


def _fox_attn(main3, maint, qaugt, kaug):
    b, s, _ = main3.shape
    trow = lambda base: pl.BlockSpec((None, LANES, s), lambda p, i: (i, base // LANES + p, 0))
    tsc = pltpu.VMEM((LANES, s), BF16)
    return pl.pallas_call(
        _fox_body,
        grid=(FOX_HEADS // 2, b),
        in_specs=[
            trow(ROW_FQ),
            pl.BlockSpec((None, s, LANES), lambda p, i: (i, 0, COL_FK // LANES + p)),
            trow(ROW_FV),
            pl.BlockSpec((None, LANES, s), lambda p, i: (i, 0, 0)),
            pl.BlockSpec((None, s, LANES), lambda p, i: (i, 0, 0)),
        ],
        out_specs=pl.BlockSpec((None, s, LANES), lambda p, i: (i, 0, p)),
        out_shape=jax.ShapeDtypeStruct((b, s, FOX_WIDTH), BF16),
        scratch_shapes=[
            pltpu.VMEM((TQ // TK, TK, TQ), F32),
            tsc, tsc, pltpu.VMEM((s, LANES), BF16), pltpu.VMEM((s, LANES), BF16), tsc, tsc,
            pltpu.VMEM((2, 2, TK, TQ), F32),
            pltpu.VMEM((2, 2, TK, TQ), BF16),
            pltpu.VMEM((2, LANES, TQ), F32),
        ],
        compiler_params=pltpu.CompilerParams(
            dimension_semantics=("arbitrary", "arbitrary"), vmem_limit_bytes=VMEM_LIMIT),
        name="fox_attn",
    )(maint, main3, maint, qaugt, kaug)


def _merge_body(x_ref, g0_ref, g1_ref, g2_ref, gb_ref, yr_ref, yd_ref, yf_ref, wr_ref, wd_ref, wf_ref, wo_ref,
                o_ref):
    def branch(g_ref, k, y_ref, w_ref):
        gate = jax.nn.sigmoid(g_ref[...].astype(F32) + gb_ref[k:k + 1, :])
        return gate * jnp.dot(y_ref[...], w_ref[...], preferred_element_type=F32)

    m = branch(g0_ref, 0, yr_ref, wr_ref) + branch(g1_ref, 1, yd_ref, wd_ref) + branch(g2_ref, 2, yf_ref, wf_ref)
    o_ref[...] = x_ref[...] + jnp.dot(m.astype(BF16), wo_ref[...], preferred_element_type=F32)


def _merge(x2, main2, gate_b, y_rnn, y_diff, y_fox, w_r, w_d, w_f, w_o):
    t, d = x2.shape
    tm = min(TM_MERGE, t)
    rows = lambda w, c=0: pl.BlockSpec((tm, w), lambda i: (i, c))
    whole = lambda a: pl.BlockSpec(a.shape, lambda i: (0, 0))
    g0 = COL_GATES // d
    return pl.pallas_call(
        _merge_body,
        grid=(t // tm,),
        in_specs=[
            rows(d), rows(d, g0), rows(d, g0 + 1), rows(d, g0 + 2), whole(gate_b),
            rows(D_RNN), rows(DIFF_WIDTH), rows(FOX_WIDTH),
            whole(w_r), whole(w_d), whole(w_f), whole(w_o),
        ],
        out_specs=rows(d),
        out_shape=jax.ShapeDtypeStruct((t, d), F32),
        compiler_params=pltpu.CompilerParams(dimension_semantics=("arbitrary",), vmem_limit_bytes=VMEM_LIMIT),
        name="merge",
    )(x2, main2, main2, main2, gate_b, y_rnn, y_diff, y_fox, w_r, w_d, w_f, w_o)


def _ffn_body(x_ref, g_ref, wg_ref, wv_ref, cwg_ref, cwv_ref, cbg_ref, cbv_ref, wd_ref, gout_ref, o_ref,
              h_scr, acc_scr, carry_g, carry_v, *, tiles_per_seq, out_norm):
    i = pl.program_id(0)
    c = pl.program_id(1)
    nc = pl.num_programs(1)
    tm = x_ref.shape[0]
    cw = wg_ref.shape[1]

    @pl.when(c == 0)
    def _():
        h_scr[...] = _rms(x_ref[...], g_ref[...]).astype(BF16)

    @pl.when((i % tiles_per_seq) == 0)
    def _():
        carry_g[c] = jnp.zeros((SUBLANES, cw), F32)
        carry_v[c] = jnp.zeros((SUBLANES, cw), F32)

    def conv(w_ref, cw_ref, cb_ref, carry_ref):
        u = jnp.dot(h_scr[...], w_ref[...], preferred_element_type=F32)
        before = carry_ref[c]
        carry_ref[c] = u[tm - SUBLANES:, :]
        return (cb_ref[...] + _shift_rows(u, before, 2) * cw_ref[0:1, :]
                + _shift_rows(u, before, 1) * cw_ref[1:2, :] + u * cw_ref[2:3, :])

    ug = conv(wg_ref, cwg_ref, cbg_ref, carry_g)
    uv = conv(wv_ref, cwv_ref, cbv_ref, carry_v)
    act = (jax.nn.gelu(ug) * uv).astype(BF16)
    part = jnp.dot(act, wd_ref[...], preferred_element_type=F32)

    @pl.when(c == 0)
    def _():
        acc_scr[...] = x_ref[...] + part

    @pl.when(c > 0)
    def _():
        acc_scr[...] += part

    @pl.when(c == nc - 1)
    def _():
        o_ref[...] = _rms(acc_scr[...], gout_ref[...]) if out_norm else acc_scr[...]


def _ffn(x2, g, w_up, conv_w, conv_b, w_down, g_out, seq_len, out_norm):
    t, d = x2.shape
    tm = min(TM_FFN, seq_len)
    cw = FF_CHUNK
    nc = D_FF // cw
    return pl.pallas_call(
        functools.partial(_ffn_body, tiles_per_seq=seq_len // tm, out_norm=out_norm),
        grid=(t // tm, nc),
        in_specs=[
            pl.BlockSpec((tm, d), lambda i, c: (i, 0)),
            pl.BlockSpec((1, d), lambda i, c: (0, 0)),
            pl.BlockSpec((d, cw), lambda i, c: (0, c)),
            pl.BlockSpec((d, cw), lambda i, c: (0, nc + c)),
            pl.BlockSpec((3, cw), lambda i, c: (0, c)),
            pl.BlockSpec((3, cw), lambda i, c: (0, nc + c)),
            pl.BlockSpec((1, cw), lambda i, c: (0, c)),
            pl.BlockSpec((1, cw), lambda i, c: (0, nc + c)),
            pl.BlockSpec((cw, d), lambda i, c: (c, 0)),
            pl.BlockSpec((1, d), lambda i, c: (0, 0)),
        ],
        out_specs=pl.BlockSpec((tm, d), lambda i, c: (i, 0)),
        out_shape=jax.ShapeDtypeStruct((t, d), F32),
        scratch_shapes=[
            pltpu.VMEM((tm, d), BF16),
            pltpu.VMEM((tm, d), F32),
            pltpu.VMEM((nc, SUBLANES, cw), F32),
            pltpu.VMEM((nc, SUBLANES, cw), F32),
        ],
        compiler_params=pltpu.CompilerParams(
            dimension_semantics=("arbitrary", "arbitrary"), vmem_limit_bytes=VMEM_LIMIT),
        name="ffn",
    )(x2, g, w_up, w_up, conv_w, conv_w, conv_b, conv_b, w_down, g_out)


def _aug_head_of_lane():
    heads = []
    for lane in range(LANES):
        group, slot = (lane % (LANES // 2)) // AUG_GROUP, lane % AUG_GROUP
        if group < FOX_HEADS // 2 and slot < AUG_SLOTS:
            heads.append(2 * group + (1 if lane < LANES // 2 else 0))
        else:
            heads.append(-1)
    return heads


def _prep_w_in(w_in_l, fox_b_f_l):
    seg = lambda a, n: w_in_l[:, a:a + n]
    w_row = jnp.concatenate([seg(W_XRNN, D_RNN), seg(W_GRNN, D_RNN), seg(W_DK, DIFF_QK), seg(W_FK, FOX_WIDTH),
                             seg(W_GATES, N_BRANCH * D_MODEL)], axis=1)
    w_t = jnp.concatenate([seg(W_DQ, DIFF_QK) * (DIFF_HEAD_DIM ** -0.5 * LOG2E),
                           seg(W_FQ, FOX_WIDTH) * (FOX_HEAD_DIM ** -0.5 * LOG2E),
                           seg(W_DV, DIFF_WIDTH), seg(W_FV, FOX_WIDTH)], axis=1).T
    heads = _aug_head_of_lane()
    idx = jnp.array([max(h, 0) for h in heads], jnp.int32)
    live = jnp.array([1.0 if h >= 0 else 0.0 for h in heads], F32)
    w_f = seg(W_FLOG, FOX_HEADS)[:, idx] * live
    fb = (fox_b_f_l[idx] * live).reshape(1, LANES)
    return w_row.astype(BF16), w_t.astype(BF16), w_f.astype(BF16), fb


def kernel(x, norm1_g, w_in, rnn_conv_w, rnn_conv_b, rg_w_r, rg_b_r, rg_w_i, rg_b_i, rg_a,
           diff_lq1, diff_lk1, diff_lq2, diff_lk2, diff_subln_g, rel_bias, fox_b_f, gate_b,
           w_br_rnn, w_br_diff, w_br_fox, w_out, norm2_g, ffn_up, ffn_conv_w, ffn_conv_b,
           ffn_down, final_g):
    bsz, s_len, d = x.shape
    t = bsz * s_len
    x2 = x.reshape(t, d)
    row = lambda v: v.reshape(1, -1)
    for l in range(DEPTH):
        w_row, w_t, w_f, fb = _prep_w_in(w_in[l], fox_b_f[l])
        main2, maint, flog = _inproj(x2, row(norm1_g[l]), w_row, w_t, w_f, bsz, s_len)
        main3 = main2.reshape(bsz, s_len, N_ROWOUT)

        y_rnn = _rnn(main3, rnn_conv_w[l], row(rnn_conv_b[l]), rg_w_r[l].astype(BF16), row(rg_b_r[l]),
                     rg_w_i[l].astype(BF16), row(rg_b_i[l]), row(rg_a[l]))

        lam_init = 0.8 - 0.6 * math.exp(-0.3 * l)
        y_diff = _diff_attn(main3, maint, rel_bias, row(diff_lq1[l]), row(diff_lk1[l]), row(diff_lq2[l]),
                            row(diff_lk2[l]), row(diff_subln_g[l]), lam_init)

        qaugt, kaug = _cum(flog.reshape(bsz, s_len, LANES), fb)
        y_fox = _fox_attn(main3, maint, qaugt, kaug)

        x2 = _merge(x2, main2, gate_b[l], y_rnn.reshape(t, D_RNN), y_diff.reshape(t, DIFF_WIDTH),
                    y_fox.reshape(t, FOX_WIDTH), w_br_rnn[l].astype(BF16), w_br_diff[l].astype(BF16),
                    w_br_fox[l].astype(BF16), w_out[l].astype(BF16))

        x2 = _ffn(x2, row(norm2_g[l]), ffn_up[l].astype(BF16), ffn_conv_w[l], row(ffn_conv_b[l]),
                  ffn_down[l].astype(BF16), row(final_g), s_len, out_norm=(l == DEPTH - 1))
    return x2.reshape(bsz, s_len, d)
```

```python
import functools
import math

import jax
import jax.numpy as jnp
from jax import lax
from jax.experimental import pallas as pl
from jax.experimental.pallas import tpu as pltpu

F32 = jnp.float32
BF16 = jnp.bfloat16

D_MODEL = 1024
DEPTH = 2
D_RNN = D_MODEL
RNN_BLOCKS = 8
RNN_BLOCK_W = D_RNN // RNN_BLOCKS
RG_LRU_C = 8.0
DIFF_HEADS = 4
DIFF_HEAD_DIM = 64
DIFF_QK = DIFF_HEADS * 2 * DIFF_HEAD_DIM
DIFF_WIDTH = DIFF_HEADS * 2 * DIFF_HEAD_DIM
FOX_HEADS = 8
FOX_HEAD_DIM = 64
FOX_WIDTH = FOX_HEADS * FOX_HEAD_DIM
N_BUCKETS = 32
MAX_EXACT = N_BUCKETS // 2
MAX_DISTANCE = 128
D_FF = ((8 * D_MODEL // 3 + 127) // 128) * 128
N_BRANCH = 3
EPS = 1e-6
LOG2E = 1.4426950408889634
NEG = -1e30

LANES = 128
SUBLANES = 8
BF16_ROWS = 16

W_XRNN = 0
W_GRNN = W_XRNN + D_RNN
W_DQ = W_GRNN + D_RNN
W_DK = W_DQ + DIFF_QK
W_DV = W_DK + DIFF_QK
W_FQ = W_DV + DIFF_WIDTH
W_FK = W_FQ + FOX_WIDTH
W_FV = W_FK + FOX_WIDTH
W_FLOG = W_FV + FOX_WIDTH
W_GATES = W_FLOG + FOX_HEADS

COL_XRNN = 0
COL_GRNN = COL_XRNN + D_RNN
COL_DK = COL_GRNN + D_RNN
COL_FK = COL_DK + DIFF_QK
COL_GATES = COL_FK + FOX_WIDTH
N_ROWOUT = COL_GATES + N_BRANCH * D_MODEL
ROW_DQ = 0
ROW_FQ = ROW_DQ + DIFF_QK
ROW_DV = ROW_FQ + FOX_WIDTH
ROW_FV = ROW_DV + DIFF_WIDTH
N_TOUT = ROW_FV + FOX_WIDTH

TQ = 512
TK = 256
TM_PROJ = 512
TN_PROJ = 1024
TM_MERGE = 512
TM_FFN = 512
MXU_TILE = 256
FF_SPLIT = (0, (D_FF // MXU_TILE + 1) // 2 * MXU_TILE, D_FF)
RNN_HALF = D_RNN // 2
VMEM_LIMIT = 56 * 1024 * 1024

AUG_GROUP = 8
AUG_SLOTS = 6
AUG_PARTS = 3


def _softplus(z):
    return jnp.maximum(z, 0.0) + jnp.log1p(jnp.exp(-jnp.abs(z)))


def _rms(x, g):
    return x * lax.rsqrt(jnp.mean(x * x, axis=-1, keepdims=True) + EPS) * g


def _sigmoid(z):
    return 0.5 * jnp.tanh(0.5 * z) + 0.5


def _shift_rows(u, before, k):
    rows, width = u.shape
    groups = rows // SUBLANES
    rot = pltpu.roll(u.reshape(groups, SUBLANES, width), k, 1)
    prev = jnp.concatenate([pltpu.roll(before, k, 0)[None], rot[:-1]], axis=0)
    sub = lax.broadcasted_iota(jnp.int32, rot.shape, 1)
    return jnp.where(sub < k, prev, rot).reshape(rows, width)


def _inproj_body(x_ref, g_ref, w_ref, wt_ref, wf_ref, out_ref, outt_ref, flog_ref, h_scr):
    h_scr[...] = _rms(x_ref[...], g_ref[...]).astype(BF16)
    flog_ref[...] = jnp.dot(h_scr[...], wf_ref[...], preferred_element_type=F32)
    for j in range(w_ref.shape[1] // TN_PROJ):
        cols = slice(j * TN_PROJ, (j + 1) * TN_PROJ)
        out_ref[:, cols] = jnp.dot(h_scr[...], w_ref[:, cols], preferred_element_type=F32).astype(out_ref.dtype)
    for j in range(wt_ref.shape[0] // TN_PROJ):
        rows = slice(j * TN_PROJ, (j + 1) * TN_PROJ)
        outt_ref[rows, :] = lax.dot_general(wt_ref[rows, :], h_scr[...], (((1,), (1,)), ((), ())),
                                            preferred_element_type=F32).astype(outt_ref.dtype)


def _resident(shape):
    return pl.BlockSpec(shape, lambda *_: (0,) * len(shape), pipeline_mode=pl.Buffered(1))


def _inproj(x2, g, w_row, w_t, w_f, bsz, s_len):
    t, d = x2.shape
    tm = min(TM_PROJ, s_len)
    tps = s_len // tm
    return pl.pallas_call(
        _inproj_body,
        grid=(t // tm,),
        in_specs=[
            pl.BlockSpec((tm, d), lambda i: (i, 0)),
            _resident(g.shape), _resident(w_row.shape), _resident(w_t.shape), _resident(w_f.shape),
        ],
        out_specs=[
            pl.BlockSpec((tm, N_ROWOUT), lambda i: (i, 0)),
            pl.BlockSpec((None, N_TOUT, tm), lambda i: (i // tps, 0, i % tps)),
            pl.BlockSpec((tm, LANES), lambda i: (i, 0)),
        ],
        out_shape=[
            jax.ShapeDtypeStruct((t, N_ROWOUT), BF16),
            jax.ShapeDtypeStruct((bsz, N_TOUT, s_len), BF16),
            jax.ShapeDtypeStruct((t, LANES), F32),
        ],
        scratch_shapes=[pltpu.VMEM((tm, d), BF16)],
        compiler_params=pltpu.CompilerParams(dimension_semantics=("arbitrary",), vmem_limit_bytes=VMEM_LIMIT),
        name="inproj",
    )(x2, g, w_row, w_t, w_f)


def _rnn_body(xr_ref, gr_ref, cw_ref, cb_ref, wr_ref, br_ref, wi_ref, bi_ref, ap_ref, y_ref, a_scr, u_scr):
    s, width = xr_ref.shape
    nblk = width // RNN_BLOCK_W
    half = pl.program_id(1)
    zeros = jnp.zeros((SUBLANES, RNN_BLOCK_W), F32)
    for n in range(nblk):
        sl = slice(n * RNN_BLOCK_W, (n + 1) * RNN_BLOCK_W)
        x = xr_ref[:, sl].astype(F32)
        xc = (cb_ref[:, sl] + _shift_rows(x, zeros, 3) * cw_ref[0:1, sl] + _shift_rows(x, zeros, 2) * cw_ref[1:2, sl]
              + _shift_rows(x, zeros, 1) * cw_ref[2:3, sl] + x * cw_ref[3:4, sl])
        xb = xc.astype(BF16)
        blk = half * nblk + n
        zr = jnp.dot(xb, wr_ref[blk], preferred_element_type=F32) + br_ref[:, sl]
        gi = _sigmoid(jnp.dot(xb, wi_ref[blk], preferred_element_type=F32) + bi_ref[:, sl])
        k = (-0.5 * RG_LRU_C) * _softplus(-ap_ref[:, sl])
        a = jnp.exp(k * jnp.tanh(0.5 * zr) + k)
        a_scr[:, sl] = a
        y = 1.0 - a * a
        u_scr[:, sl] = jnp.where(y > 0.0, y * lax.rsqrt(y), 0.0) * (gi * xc)

    row8 = lax.broadcasted_iota(jnp.int32, (SUBLANES, width), 0)

    def group(k, hprev):
        off = pl.multiple_of(k * SUBLANES, SUBLANES)
        a = a_scr[pl.ds(off, SUBLANES), :]
        u = u_scr[pl.ds(off, SUBLANES), :]
        for d in (1, 2, 4):
            a_sh = jnp.where(row8 >= d, pltpu.roll(a, d, 0), 1.0)
            u_sh = jnp.where(row8 >= d, pltpu.roll(u, d, 0), 0.0)
            u = a * u_sh + u
            a = a * a_sh
        h = a * hprev + u
        u_scr[pl.ds(off, SUBLANES), :] = h
        return h[SUBLANES - 1:SUBLANES, :]

    lax.fori_loop(0, s // SUBLANES, group, jnp.zeros((1, width), F32))

    for n in range(nblk):
        sl = slice(n * RNN_BLOCK_W, (n + 1) * RNN_BLOCK_W)
        y_ref[:, sl] = (jax.nn.gelu(gr_ref[:, sl].astype(F32)) * u_scr[:, sl]).astype(y_ref.dtype)


def _rnn(main3, conv_w, conv_b, w_r, b_r, w_i, b_i, a_param):
    b, s, _ = main3.shape
    w = RNN_HALF
    nh = D_RNN // w
    vec = lambda k: pl.BlockSpec((k, w), lambda i, j: (0, j))
    full3 = pl.BlockSpec((RNN_BLOCKS, RNN_BLOCK_W, RNN_BLOCK_W), lambda i, j: (0, 0, 0))
    return pl.pallas_call(
        _rnn_body,
        grid=(b, nh),
        in_specs=[
            pl.BlockSpec((None, s, w), lambda i, j: (i, 0, COL_XRNN // w + j)),
            pl.BlockSpec((None, s, w), lambda i, j: (i, 0, COL_GRNN // w + j)),
            vec(4), vec(1), full3, vec(1), full3, vec(1), vec(1),
        ],
        out_specs=pl.BlockSpec((None, s, w), lambda i, j: (i, 0, j)),
        out_shape=jax.ShapeDtypeStruct((b, s, D_RNN), BF16),
        scratch_shapes=[pltpu.VMEM((s, w), F32), pltpu.VMEM((s, w), F32)],
        compiler_params=pltpu.CompilerParams(
            dimension_semantics=("arbitrary", "arbitrary"), vmem_limit_bytes=VMEM_LIMIT),
        name="rnn",
    )(main3, main3, conv_w, conv_b, w_r, b_r, w_i, b_i, a_param)


def _cum_body(flog_ref, fb_ref, qaugt_ref, kaug_ref):
    s = flog_ref.shape[0]
    z = flog_ref[...] + fb_ref[...]
    c = -_softplus(-z) * LOG2E
    row = lax.broadcasted_iota(jnp.int32, c.shape, 0)
    d = 1
    while d < s:
        c = c + jnp.where(row >= d, pltpu.roll(c, d, 0), 0.0)
        d *= 2
    hi = c.astype(BF16).astype(F32)
    mid = (c - hi).astype(BF16).astype(F32)
    lo = (c - hi - mid).astype(BF16).astype(F32)
    slot = lax.broadcasted_iota(jnp.int32, c.shape, 1) % AUG_GROUP
    part = jnp.where(slot % AUG_PARTS == 0, hi, jnp.where(slot % AUG_PARTS == 1, mid, lo))
    used = slot < AUG_SLOTS
    qside = slot < AUG_PARTS
    qaug = jnp.where(used, jnp.where(qside, part, 1.0), 0.0)
    kaug = jnp.where(used, jnp.where(qside, 1.0, -part), 0.0)
    qaugt_ref[...] = qaug.T.astype(BF16)
    kaug_ref[...] = kaug.astype(BF16)


def _cum(flog3, fb):
    b, s, _ = flog3.shape
    return pl.pallas_call(
        _cum_body,
        grid=(b,),
        in_specs=[
            pl.BlockSpec((None, s, LANES), lambda i: (i, 0, 0)),
            pl.BlockSpec((1, LANES), lambda i: (0, 0)),
        ],
        out_specs=[
            pl.BlockSpec((None, LANES, s), lambda i: (i, 0, 0)),
            pl.BlockSpec((None, s, LANES), lambda i: (i, 0, 0)),
        ],
        out_shape=[
            jax.ShapeDtypeStruct((b, LANES, s), BF16),
            jax.ShapeDtypeStruct((b, s, LANES), BF16),
        ],
        compiler_params=pltpu.CompilerParams(dimension_semantics=("arbitrary",)),
        name="fox_cumsum",
    )(flog3, fb)


def _attend(q_tile, k_tile, vt_tile, n_far, near, sc_scr, p_scr, acc_scr):
    n_tiles = n_far + len(near)
    assert [j for j, _, _ in near] == list(range(n_far, n_tiles))
    maps = range(2)
    tq = sc_scr.shape[-1]
    FAR = (None, 0)
    kind = lambda t: FAR if t < n_far else near[t - n_far][1:]

    def scores(t, slot, tile):
        bias, lo = tile
        tops = []
        for w in maps:
            sc = jnp.dot(k_tile(t, w), q_tile(w)[:, lo:], preferred_element_type=F32)
            if bias is not None:
                sc = sc + bias[:, lo:]
            sc_scr[slot, w, :, lo:] = sc
            tops.append(jnp.max(sc, axis=0, keepdims=True))
        return tuple(tops)

    def values(t, slot, alphas, tile):
        lo = tile[1]
        for w in maps:
            acc_scr[w, :, lo:] = alphas[w] * acc_scr[w, :, lo:] + jnp.dot(
                vt_tile(t, w), p_scr[slot, w, :, lo:], preferred_element_type=F32)

    def iteration(t, slot, ms, alphas, tops, prev, cur, nxt):
        if prev is not None:
            values(t - 1, 1 - slot, alphas, prev)
        next_tops = scores(t + 1, 1 - slot, nxt) if nxt is not None else tops
        lo = cur[1]
        new_ms, new_alphas = [], []
        for w in maps:
            m_old = ms[w][:, lo:]
            m_new = jnp.maximum(m_old, tops[w])
            new_alphas.append(jnp.exp2(m_old - m_new))
            p_scr[slot, w, :, lo:] = jnp.exp2(sc_scr[slot, w, :, lo:] - m_new).astype(BF16)
            new_ms.append(m_new if lo == 0 else jnp.concatenate([ms[w][:, :lo], m_new], axis=1))
        return tuple(new_ms), tuple(new_alphas), next_tops

    for w in maps:
        acc_scr[w] = jnp.zeros(acc_scr.shape[1:], F32)
    ms = (jnp.full((1, tq), NEG, F32),) * 2
    tops = scores(0, 0, kind(0))
    ms, alphas, tops = iteration(0, 0, ms, None, tops, None, kind(0), kind(1) if n_tiles > 1 else None)
    t = 1
    n_pairs = max(n_far - 2, 0) // 2
    if n_pairs > 0:
        def pair(k, st):
            st = iteration(1 + 2 * k, 1, *st, FAR, FAR, FAR)
            return iteration(2 + 2 * k, 0, *st, FAR, FAR, FAR)
        ms, alphas, tops = lax.fori_loop(0, n_pairs, pair, (ms, alphas, tops))
        t = 1 + 2 * n_pairs
    while t < n_tiles:
        ms, alphas, tops = iteration(t, t % 2, ms, alphas, tops, kind(t - 1), kind(t),
                                     kind(t + 1) if t + 1 < n_tiles else None)
        t += 1
    values(n_tiles - 1, (n_tiles - 1) % 2, alphas, kind(n_tiles - 1))
    return acc_scr[0], acc_scr[1]


def _diff_body(rel_ref, qt_ref, k_ref, vt_ref, lq1_ref, lk1_ref, lq2_ref, lk2_ref, sg_ref, o_ref,
               bias_scr, qa_scr, qb_scr, vx_scr, sc_scr, p_scr, acc_scr, *, lam_init):
    s = k_ref.shape[0]
    head = pl.program_id(0)

    @pl.when(pl.program_id(1) == 0)
    def _():
        ki = lax.broadcasted_iota(jnp.int32, (TK, TQ), 0)
        qi = lax.broadcasted_iota(jnp.int32, (TK, TQ), 1)
        far = rel_ref[N_BUCKETS - 1, head]
        for idx in range(3):
            dist = qi - ki + (1 - idx) * TK
            n = jnp.maximum(dist, 0)
            nf = jnp.maximum(n, 1).astype(F32)
            large = MAX_EXACT + (jnp.log(nf / MAX_EXACT) / math.log(MAX_DISTANCE / MAX_EXACT)
                                 * (N_BUCKETS - MAX_EXACT)).astype(jnp.int32)
            large = jnp.minimum(large, N_BUCKETS - 1)
            bucket = jnp.where(n < MAX_EXACT, n, large)
            bias = jnp.zeros((TK, TQ), F32)
            for bk in range(N_BUCKETS):
                bias = jnp.where(bucket == bk, rel_ref[bk, head], bias)
            bias_scr[idx] = jnp.where(dist >= 0, (bias - far) * LOG2E, NEG)

    rowi = lax.broadcasted_iota(jnp.int32, qt_ref.shape, 0)
    qt = qt_ref[...].astype(F32)
    qa_scr[...] = jnp.where(rowi < DIFF_HEAD_DIM, qt, 0.0).astype(BF16)
    qb_scr[...] = jnp.where(rowi < DIFF_HEAD_DIM, 0.0, qt).astype(BF16)
    vx_scr[0:LANES, :] = vt_ref[...]
    vx_scr[LANES:, :] = jnp.ones((BF16_ROWS, s), BF16)

    lam = (jnp.exp(jnp.sum(lq1_ref[...] * lk1_ref[...], axis=-1, keepdims=True))
           - jnp.exp(jnp.sum(lq2_ref[...] * lk2_ref[...], axis=-1, keepdims=True)) + lam_init)

    k_tile = lambda j, w: k_ref[pl.ds(j * TK, TK), :]
    vt_tile = lambda j, w: vx_scr[:, pl.ds(j * TK, TK)]
    for i in range(s // TQ):
        qsl = slice(i * TQ, (i + 1) * TQ)
        j0 = i * (TQ // TK)
        near = [(j0 - 1 + idx, bias_scr[idx], max(idx - 1, 0) * TK) for idx in range(3) if j0 - 1 + idx >= 0]
        n_far = max(j0 - 1, 0)
        q_tile = lambda w, qsl=qsl: (qa_scr, qb_scr)[w][:, qsl]
        acca, accb = _attend(q_tile, k_tile, vt_tile, n_far, near, sc_scr, p_scr, acc_scr)
        o = (acca[:LANES] / acca[LANES:LANES + 1] - lam * (accb[:LANES] / accb[LANES:LANES + 1]))
        o = o * lax.rsqrt(jnp.mean(o * o, axis=0, keepdims=True) + EPS)
        o_ref[qsl, :] = (o.T * sg_ref[...] * (1.0 - lam_init)).astype(o_ref.dtype)


def _diff_attn(main3, maint, rel_bias, lq1, lk1, lq2, lk2, subln_g, lam_init):
    b, s, _ = main3.shape
    small = lambda w: pl.BlockSpec((1, w), lambda h, i: (0, 0))
    trow = lambda base: pl.BlockSpec((None, LANES, s), lambda h, i: (i, base // LANES + h, 0))
    return pl.pallas_call(
        functools.partial(_diff_body, lam_init=lam_init),
        grid=(DIFF_HEADS, b),
        in_specs=[
            pl.BlockSpec(memory_space=pltpu.SMEM),
            trow(ROW_DQ),
            pl.BlockSpec((None, s, LANES), lambda h, i: (i, 0, COL_DK // LANES + h)),
            trow(ROW_DV),
            small(DIFF_HEAD_DIM), small(DIFF_HEAD_DIM), small(DIFF_HEAD_DIM), small(DIFF_HEAD_DIM),
            small(2 * DIFF_HEAD_DIM),
        ],
        out_specs=pl.BlockSpec((None, s, LANES), lambda h, i: (i, 0, h)),
        out_shape=jax.ShapeDtypeStruct((b, s, DIFF_WIDTH), BF16),
        scratch_shapes=[
            pltpu.VMEM((3, TK, TQ), F32),
            pltpu.VMEM((LANES, s), BF16),
            pltpu.VMEM((LANES, s), BF16),
            pltpu.VMEM((LANES + BF16_ROWS, s), BF16),
            pltpu.VMEM((2, 2, TK, TQ), F32),
            pltpu.VMEM((2, 2, TK, TQ), BF16),
            pltpu.VMEM((2, LANES + BF16_ROWS, TQ), F32),
        ],
        compiler_params=pltpu.CompilerParams(
            dimension_semantics=("arbitrary", "arbitrary"), vmem_limit_bytes=VMEM_LIMIT),
        name="diff_attn",
    )(rel_bias, maint, main3, maint, lq1, lk1, lq2, lk2, subln_g)


def _fox_body(qt_ref, k_ref, vt_ref, qaugt_ref, kaug_ref, o_ref,
              mask_scr, qa_scr, qb_scr, ka_scr, kb_scr, va_scr, vb_scr, sc_scr, p_scr, acc_scr):
    s = k_ref.shape[0]
    pair = pl.program_id(0)

    @pl.when(pl.program_id(1) == 0)
    def _():
        ki = lax.broadcasted_iota(jnp.int32, (TK, TQ), 0)
        qi = lax.broadcasted_iota(jnp.int32, (TK, TQ), 1)
        for idx in range(TQ // TK):
            mask_scr[idx] = jnp.where(qi >= ki + idx * TK, 0.0, NEG)

    half = FOX_HEAD_DIM
    lo_a, lo_b = half + AUG_GROUP * pair, AUG_GROUP * pair

    def split(x, aug, idx):
        is_a = idx < half
        aug_a = jnp.where((idx >= lo_a) & (idx < lo_a + AUG_SLOTS), aug, 0.0)
        aug_b = jnp.where((idx >= lo_b) & (idx < lo_b + AUG_SLOTS), aug, 0.0)
        return jnp.where(is_a, x, aug_a).astype(BF16), jnp.where(is_a, aug_b, x).astype(BF16)

    rowi = lax.broadcasted_iota(jnp.int32, qt_ref.shape, 0)
    lanei = lax.broadcasted_iota(jnp.int32, k_ref.shape, 1)
    qa_scr[...], qb_scr[...] = split(qt_ref[...].astype(F32), qaugt_ref[...].astype(F32), rowi)
    ka_scr[...], kb_scr[...] = split(k_ref[...].astype(F32), kaug_ref[...].astype(F32), lanei)
    vt = vt_ref[...].astype(F32)
    va_scr[...] = jnp.where(rowi < half, vt, 1.0).astype(BF16)
    vb_scr[...] = jnp.where(rowi < half, 1.0, vt).astype(BF16)

    k_tile = lambda j, w: (ka_scr, kb_scr)[w][pl.ds(j * TK, TK), :]
    vt_tile = lambda j, w: (va_scr, vb_scr)[w][:, pl.ds(j * TK, TK)]
    rows = lax.broadcasted_iota(jnp.int32, (LANES, TQ), 0)
    for i in range(s // TQ):
        qsl = slice(i * TQ, (i + 1) * TQ)
        j0 = i * (TQ // TK)
        near = [(j0 + idx, mask_scr[idx], idx * TK) for idx in range(TQ // TK)]
        q_tile = lambda w, qsl=qsl: (qa_scr, qb_scr)[w][:, qsl]
        acca, accb = _attend(q_tile, k_tile, vt_tile, j0, near, sc_scr, p_scr, acc_scr)
        o = jnp.where(rows < half, acca / acca[half:half + 1], accb / accb[0:1])
        o_ref[qsl, :] = o.T.astype(o_ref.dtype)


def _fox_attn(main3, maint, qaugt, kaug):
    b, s, _ = main3.shape
    trow = lambda base: pl.BlockSpec((None, LANES, s), lambda p, i: (i, base // LANES + p, 0))
    tsc = pltpu.VMEM((LANES, s), BF16)
    return pl.pallas_call(
        _fox_body,
        grid=(FOX_HEADS // 2, b),
        in_specs=[
            trow(ROW_FQ),
            pl.BlockSpec((None, s, LANES), lambda p, i: (i, 0, COL_FK // LANES + p)),
            trow(ROW_FV),
            pl.BlockSpec((None, LANES, s), lambda p, i: (i, 0, 0)),
            pl.BlockSpec((None, s, LANES), lambda p, i: (i, 0, 0)),
        ],
        out_specs=pl.BlockSpec((None, s, LANES), lambda p, i: (i, 0, p)),
        out_shape=jax.ShapeDtypeStruct((b, s, FOX_WIDTH), BF16),
        scratch_shapes=[
            pltpu.VMEM((TQ // TK, TK, TQ), F32),
            tsc, tsc, pltpu.VMEM((s, LANES), BF16), pltpu.VMEM((s, LANES), BF16), tsc, tsc,
            pltpu.VMEM((2, 2, TK, TQ), F32),
            pltpu.VMEM((2, 2, TK, TQ), BF16),
            pltpu.VMEM((2, LANES, TQ), F32),
        ],
        compiler_params=pltpu.CompilerParams(
            dimension_semantics=("arbitrary", "arbitrary"), vmem_limit_bytes=VMEM_LIMIT),
        name="fox_attn",
    )(maint, main3, maint, qaugt, kaug)


def _merge_body(x_ref, g0_ref, g1_ref, g2_ref, gb_ref, yr_ref, yd_ref, yf_ref, wr_ref, wd_ref, wf_ref, wo_ref,
                o_ref):
    def branch(g_ref, k, y_ref, w_ref):
        gate = jax.nn.sigmoid(g_ref[...].astype(F32) + gb_ref[k:k + 1, :])
        return gate * jnp.dot(y_ref[...], w_ref[...], preferred_element_type=F32)

    m = branch(g0_ref, 0, yr_ref, wr_ref) + branch(g1_ref, 1, yd_ref, wd_ref) + branch(g2_ref, 2, yf_ref, wf_ref)
    o_ref[...] = x_ref[...] + jnp.dot(m.astype(BF16), wo_ref[...], preferred_element_type=F32)


def _merge(x2, main2, gate_b, y_rnn, y_diff, y_fox, w_r, w_d, w_f, w_o):
    t, d = x2.shape
    tm = min(TM_MERGE, t)
    rows = lambda w, c=0: pl.BlockSpec((tm, w), lambda i: (i, c))
    whole = lambda a: pl.BlockSpec(a.shape, lambda i: (0, 0))
    g0 = COL_GATES // d
    return pl.pallas_call(
        _merge_body,
        grid=(t // tm,),
        in_specs=[
            rows(d), rows(d, g0), rows(d, g0 + 1), rows(d, g0 + 2), whole(gate_b),
            rows(D_RNN), rows(DIFF_WIDTH), rows(FOX_WIDTH),
            whole(w_r), whole(w_d), whole(w_f), whole(w_o),
        ],
        out_specs=rows(d),
        out_shape=jax.ShapeDtypeStruct((t, d), F32),
        compiler_params=pltpu.CompilerParams(dimension_semantics=("arbitrary",), vmem_limit_bytes=VMEM_LIMIT),
        name="merge",
    )(x2, main2, main2, main2, gate_b, y_rnn, y_diff, y_fox, w_r, w_d, w_f, w_o)


def _ffn_body(x_ref, g_ref, wu_ref, cw_ref, cb_ref, wd_ref, gout_ref, o_ref,
              h_scr, acc_scr, act_scr, carry_scr, *, n_tiles, tiles_per_seq, out_norm):
    i = pl.program_id(0)
    tm = x_ref.shape[0]
    part = lambda c, off=0: pl.ds(off + FF_SPLIT[c], FF_SPLIT[c + 1] - FF_SPLIT[c])

    @pl.when(i == 0)
    def _():
        act_scr[...] = jnp.zeros(act_scr.shape, BF16)
        acc_scr[...] = jnp.zeros(acc_scr.shape, F32)

    @pl.when((jnp.minimum(i, n_tiles - 1) % tiles_per_seq) == 0)
    def _():
        carry_scr[...] = jnp.zeros(carry_scr.shape, F32)

    h_scr[...] = _rms(x_ref[...], g_ref[...]).astype(BF16)

    def up(c, off):
        return jnp.dot(h_scr[...], wu_ref[:, part(c, off)], preferred_element_type=F32)

    def activation(us, c):
        out = []
        for u, off in zip(us, (0, D_FF)):
            cols = part(c, off)
            before = carry_scr[:, cols]
            carry_scr[:, cols] = u[tm - SUBLANES:, :]
            out.append(cb_ref[:, cols] + _shift_rows(u, before, 2) * cw_ref[0:1, cols]
                       + _shift_rows(u, before, 1) * cw_ref[1:2, cols] + u * cw_ref[2:3, cols])
        return (jax.nn.gelu(out[0]) * out[1]).astype(BF16)

    down = lambda act, c: jnp.dot(act, wd_ref[part(c), :], preferred_element_type=F32)

    ug0 = up(0, 0)
    done = acc_scr[...] + down(act_scr[...], 1)
    o_ref[...] = _rms(done, gout_ref[...]) if out_norm else done
    uv0 = up(0, D_FF)
    act0 = activation((ug0, uv0), 0)
    ug1 = up(1, 0)
    acc_scr[...] = x_ref[...] + down(act0, 0)
    uv1 = up(1, D_FF)
    act_scr[...] = activation((ug1, uv1), 1)


def _ffn(x2, g, w_up, conv_w, conv_b, w_down, g_out, seq_len, out_norm):
    t, d = x2.shape
    tm = min(TM_FFN, seq_len)
    n_tiles = t // tm
    return pl.pallas_call(
        functools.partial(_ffn_body, n_tiles=n_tiles, tiles_per_seq=seq_len // tm, out_norm=out_norm),
        grid=(n_tiles + 1,),
        in_specs=[
            pl.BlockSpec((tm, d), lambda i: (jnp.minimum(i, n_tiles - 1), 0)),
            _resident(g.shape), _resident(w_up.shape), _resident(conv_w.shape), _resident(conv_b.shape),
            _resident(w_down.shape), _resident(g_out.shape),
        ],
        out_specs=pl.BlockSpec((tm, d), lambda i: (jnp.maximum(i - 1, 0), 0)),
        out_shape=jax.ShapeDtypeStruct((t, d), F32),
        scratch_shapes=[
            pltpu.VMEM((tm, d), BF16),
            pltpu.VMEM((tm, d), F32),
            pltpu.VMEM((tm, FF_SPLIT[2] - FF_SPLIT[1]), BF16),
            pltpu.VMEM((SUBLANES, 2 * D_FF), F32),
        ],
        compiler_params=pltpu.CompilerParams(dimension_semantics=("arbitrary",), vmem_limit_bytes=VMEM_LIMIT),
        name="ffn",
    )(x2, g, w_up, conv_w, conv_b, w_down, g_out)


def _aug_head_of_lane():
    heads = []
    for lane in range(LANES):
        group, slot = (lane % (LANES // 2)) // AUG_GROUP, lane % AUG_GROUP
        if group < FOX_HEADS // 2 and slot < AUG_SLOTS:
            heads.append(2 * group + (1 if lane < LANES // 2 else 0))
        else:
            heads.append(-1)
    return heads


def _prep_w_in(w_in, fox_b_f):
    colscale = jnp.ones((w_in.shape[-1],), F32)
    colscale = colscale.at[W_DQ:W_DQ + DIFF_QK].set(DIFF_HEAD_DIM ** -0.5 * LOG2E)
    colscale = colscale.at[W_FQ:W_FQ + FOX_WIDTH].set(FOX_HEAD_DIM ** -0.5 * LOG2E)
    wb = (w_in * colscale).astype(BF16)
    heads = _aug_head_of_lane()
    idx = jnp.array([max(h, 0) for h in heads], jnp.int32)
    live = jnp.array([1.0 if h >= 0 else 0.0 for h in heads], F32)
    out = []
    for l in range(w_in.shape[0]):
        seg = lambda a, n, l=l: wb[l, :, a:a + n]
        w_row = jnp.concatenate([seg(W_XRNN, D_RNN), seg(W_GRNN, D_RNN), seg(W_DK, DIFF_QK), seg(W_FK, FOX_WIDTH),
                                 seg(W_GATES, N_BRANCH * D_MODEL)], axis=1)
        w_t = jnp.concatenate([seg(W_DQ, DIFF_QK), seg(W_FQ, FOX_WIDTH), seg(W_DV, DIFF_WIDTH),
                               seg(W_FV, FOX_WIDTH)], axis=1).T
        w_f = (w_in[l, :, W_FLOG:W_FLOG + FOX_HEADS][:, idx] * live).astype(BF16)
        fb = (fox_b_f[l][idx] * live).reshape(1, LANES)
        out.append((w_row, w_t, w_f, fb))
    return out


def kernel(x, norm1_g, w_in, rnn_conv_w, rnn_conv_b, rg_w_r, rg_b_r, rg_w_i, rg_b_i, rg_a,
           diff_lq1, diff_lk1, diff_lq2, diff_lk2, diff_subln_g, rel_bias, fox_b_f, gate_b,
           w_br_rnn, w_br_diff, w_br_fox, w_out, norm2_g, ffn_up, ffn_conv_w, ffn_conv_b,
           ffn_down, final_g):
    bsz, s_len, d = x.shape
    t = bsz * s_len
    x2 = x.reshape(t, d)
    row = lambda v: v.reshape(1, -1)
    w_in_parts = _prep_w_in(w_in, fox_b_f)
    rg_w_r, rg_w_i, w_br_rnn, w_br_diff, w_br_fox, w_out, ffn_up, ffn_down = (
        w.astype(BF16) for w in (rg_w_r, rg_w_i, w_br_rnn, w_br_diff, w_br_fox, w_out, ffn_up, ffn_down))
    for l in range(DEPTH):
        w_row, w_t, w_f, fb = w_in_parts[l]
        main2, maint, flog = _inproj(x2, row(norm1_g[l]), w_row, w_t, w_f, bsz, s_len)
        main3 = main2.reshape(bsz, s_len, N_ROWOUT)

        y_rnn = _rnn(main3, rnn_conv_w[l], row(rnn_conv_b[l]), rg_w_r[l], row(rg_b_r[l]),
                     rg_w_i[l], row(rg_b_i[l]), row(rg_a[l]))

        lam_init = 0.8 - 0.6 * math.exp(-0.3 * l)
        y_diff = _diff_attn(main3, maint, rel_bias, row(diff_lq1[l]), row(diff_lk1[l]), row(diff_lq2[l]),
                            row(diff_lk2[l]), row(diff_subln_g[l]), lam_init)

        qaugt, kaug = _cum(flog.reshape(bsz, s_len, LANES), fb)
        y_fox = _fox_attn(main3, maint, qaugt, kaug)

        x2 = _merge(x2, main2, gate_b[l], y_rnn.reshape(t, D_RNN), y_diff.reshape(t, DIFF_WIDTH),
                    y_fox.reshape(t, FOX_WIDTH), w_br_rnn[l], w_br_diff[l], w_br_fox[l], w_out[l])

        x2 = _ffn(x2, row(norm2_g[l]), ffn_up[l], ffn_conv_w[l], row(ffn_conv_b[l]),
                  ffn_down[l], row(final_g), s_len, out_norm=(l == DEPTH - 1))
    return x2.reshape(bsz, s_len, d)
```

```python
import functools
import math

import jax
import jax.numpy as jnp
from jax import lax
from jax.experimental import pallas as pl
from jax.experimental.pallas import tpu as pltpu

F32 = jnp.float32
BF16 = jnp.bfloat16

D_MODEL = 1024
DEPTH = 2
D_RNN = D_MODEL
RNN_BLOCKS = 8
RNN_BLOCK_W = D_RNN // RNN_BLOCKS
RG_LRU_C = 8.0
DIFF_HEADS = 4
DIFF_HEAD_DIM = 64
DIFF_QK = DIFF_HEADS * 2 * DIFF_HEAD_DIM
DIFF_WIDTH = DIFF_HEADS * 2 * DIFF_HEAD_DIM
FOX_HEADS = 8
FOX_HEAD_DIM = 64
FOX_WIDTH = FOX_HEADS * FOX_HEAD_DIM
N_BUCKETS = 32
MAX_EXACT = N_BUCKETS // 2
MAX_DISTANCE = 128
D_FF = ((8 * D_MODEL // 3 + 127) // 128) * 128
N_BRANCH = 3
EPS = 1e-6
LOG2E = 1.4426950408889634
NEG = -1e30

LANES = 128
SUBLANES = 8
BF16_ROWS = 16

W_XRNN = 0
W_GRNN = W_XRNN + D_RNN
W_DQ = W_GRNN + D_RNN
W_DK = W_DQ + DIFF_QK
W_DV = W_DK + DIFF_QK
W_FQ = W_DV + DIFF_WIDTH
W_FK = W_FQ + FOX_WIDTH
W_FV = W_FK + FOX_WIDTH
W_FLOG = W_FV + FOX_WIDTH
W_GATES = W_FLOG + FOX_HEADS

COL_XRNN = 0
COL_GRNN = COL_XRNN + D_RNN
COL_DK = COL_GRNN + D_RNN
COL_FK = COL_DK + DIFF_QK
COL_GATES = COL_FK + FOX_WIDTH
N_ROWOUT = COL_GATES + N_BRANCH * D_MODEL
ROW_DQ = 0
ROW_FQ = ROW_DQ + DIFF_QK
ROW_DV = ROW_FQ + FOX_WIDTH
ROW_FV = ROW_DV + DIFF_WIDTH
N_TOUT = ROW_FV + FOX_WIDTH

TQ = 512
TK = 256
TM_PROJ = 512
TN_PROJ = 1024
TM_MERGE = 512
TM_FFN = 512
MXU_TILE = 256
FF_SPLIT = (0, (D_FF // MXU_TILE + 1) // 2 * MXU_TILE, D_FF)
RNN_HALF = D_RNN // 2
VMEM_LIMIT = 56 * 1024 * 1024

AUG_GROUP = 8
AUG_SLOTS = 6
AUG_PARTS = 3


def _softplus(z):
    return jnp.maximum(z, 0.0) + jnp.log1p(jnp.exp(-jnp.abs(z)))


def _rms(x, g):
    return x * lax.rsqrt(jnp.mean(x * x, axis=-1, keepdims=True) + EPS) * g


def _sigmoid(z):
    return 0.5 * jnp.tanh(0.5 * z) + 0.5


def _shift_rows(u, before, k):
    rows, width = u.shape
    groups = rows // SUBLANES
    rot = pltpu.roll(u.reshape(groups, SUBLANES, width), k, 1)
    prev = jnp.concatenate([pltpu.roll(before, k, 0)[None], rot[:-1]], axis=0)
    sub = lax.broadcasted_iota(jnp.int32, rot.shape, 1)
    return jnp.where(sub < k, prev, rot).reshape(rows, width)


def _inproj_body(x_ref, g_ref, w_ref, wt_ref, wf_ref, out_ref, outt_ref, flog_ref, h_scr):
    h_scr[...] = _rms(x_ref[...], g_ref[...]).astype(BF16)
    flog_ref[...] = jnp.dot(h_scr[...], wf_ref[...], preferred_element_type=F32)
    for j in range(w_ref.shape[1] // TN_PROJ):
        cols = slice(j * TN_PROJ, (j + 1) * TN_PROJ)
        out_ref[:, cols] = jnp.dot(h_scr[...], w_ref[:, cols], preferred_element_type=F32).astype(out_ref.dtype)
    for j in range(wt_ref.shape[0] // TN_PROJ):
        rows = slice(j * TN_PROJ, (j + 1) * TN_PROJ)
        outt_ref[rows, :] = lax.dot_general(wt_ref[rows, :], h_scr[...], (((1,), (1,)), ((), ())),
                                            preferred_element_type=F32).astype(outt_ref.dtype)


def _resident(shape):
    return pl.BlockSpec(shape, lambda *_: (0,) * len(shape), pipeline_mode=pl.Buffered(1))


def _inproj(x2, g, w_row, w_t, w_f, bsz, s_len):
    t, d = x2.shape
    tm = min(TM_PROJ, s_len)
    tps = s_len // tm
    return pl.pallas_call(
        _inproj_body,
        grid=(t // tm,),
        in_specs=[
            pl.BlockSpec((tm, d), lambda i: (i, 0)),
            _resident(g.shape), _resident(w_row.shape), _resident(w_t.shape), _resident(w_f.shape),
        ],
        out_specs=[
            pl.BlockSpec((tm, N_ROWOUT), lambda i: (i, 0)),
            pl.BlockSpec((None, N_TOUT, tm), lambda i: (i // tps, 0, i % tps)),
            pl.BlockSpec((tm, LANES), lambda i: (i, 0)),
        ],
        out_shape=[
            jax.ShapeDtypeStruct((t, N_ROWOUT), BF16),
            jax.ShapeDtypeStruct((bsz, N_TOUT, s_len), BF16),
            jax.ShapeDtypeStruct((t, LANES), F32),
        ],
        scratch_shapes=[pltpu.VMEM((tm, d), BF16)],
        compiler_params=pltpu.CompilerParams(dimension_semantics=("arbitrary",), vmem_limit_bytes=VMEM_LIMIT),
        name="inproj",
    )(x2, g, w_row, w_t, w_f)


def _rnn_body(xr_ref, gr_ref, cw_ref, cb_ref, wr_ref, br_ref, wi_ref, bi_ref, ap_ref, y_ref, a_scr, u_scr):
    s, width = xr_ref.shape
    nblk = width // RNN_BLOCK_W
    half = pl.program_id(1)
    zeros = jnp.zeros((SUBLANES, RNN_BLOCK_W), F32)
    for n in range(nblk):
        sl = slice(n * RNN_BLOCK_W, (n + 1) * RNN_BLOCK_W)
        x = xr_ref[:, sl].astype(F32)
        xc = (cb_ref[:, sl] + _shift_rows(x, zeros, 3) * cw_ref[0:1, sl] + _shift_rows(x, zeros, 2) * cw_ref[1:2, sl]
              + _shift_rows(x, zeros, 1) * cw_ref[2:3, sl] + x * cw_ref[3:4, sl])
        xb = xc.astype(BF16)
        blk = half * nblk + n
        zr = jnp.dot(xb, wr_ref[blk], preferred_element_type=F32) + br_ref[:, sl]
        gi = _sigmoid(jnp.dot(xb, wi_ref[blk], preferred_element_type=F32) + bi_ref[:, sl])
        k = (-0.5 * RG_LRU_C) * _softplus(-ap_ref[:, sl])
        a = jnp.exp(k * jnp.tanh(0.5 * zr) + k)
        a_scr[:, sl] = a
        y = 1.0 - a * a
        u_scr[:, sl] = jnp.where(y > 0.0, y * lax.rsqrt(y), 0.0) * (gi * xc)

    row8 = lax.broadcasted_iota(jnp.int32, (SUBLANES, width), 0)

    def group(k, hprev):
        off = pl.multiple_of(k * SUBLANES, SUBLANES)
        a = a_scr[pl.ds(off, SUBLANES), :]
        u = u_scr[pl.ds(off, SUBLANES), :]
        for d in (1, 2, 4):
            a_sh = jnp.where(row8 >= d, pltpu.roll(a, d, 0), 1.0)
            u_sh = jnp.where(row8 >= d, pltpu.roll(u, d, 0), 0.0)
            u = a * u_sh + u
            a = a * a_sh
        h = a * hprev + u
        u_scr[pl.ds(off, SUBLANES), :] = h
        return h[SUBLANES - 1:SUBLANES, :]

    lax.fori_loop(0, s // SUBLANES, group, jnp.zeros((1, width), F32))

    for n in range(nblk):
        sl = slice(n * RNN_BLOCK_W, (n + 1) * RNN_BLOCK_W)
        y_ref[:, sl] = (jax.nn.gelu(gr_ref[:, sl].astype(F32)) * u_scr[:, sl]).astype(y_ref.dtype)


def _rnn(main3, conv_w, conv_b, w_r, b_r, w_i, b_i, a_param):
    b, s, _ = main3.shape
    w = RNN_HALF
    nh = D_RNN // w
    vec = lambda k: pl.BlockSpec((k, w), lambda i, j: (0, j))
    full3 = pl.BlockSpec((RNN_BLOCKS, RNN_BLOCK_W, RNN_BLOCK_W), lambda i, j: (0, 0, 0))
    return pl.pallas_call(
        _rnn_body,
        grid=(b, nh),
        in_specs=[
            pl.BlockSpec((None, s, w), lambda i, j: (i, 0, COL_XRNN // w + j)),
            pl.BlockSpec((None, s, w), lambda i, j: (i, 0, COL_GRNN // w + j)),
            vec(4), vec(1), full3, vec(1), full3, vec(1), vec(1),
        ],
        out_specs=pl.BlockSpec((None, s, w), lambda i, j: (i, 0, j)),
        out_shape=jax.ShapeDtypeStruct((b, s, D_RNN), BF16),
        scratch_shapes=[pltpu.VMEM((s, w), F32), pltpu.VMEM((s, w), F32)],
        compiler_params=pltpu.CompilerParams(
            dimension_semantics=("arbitrary", "arbitrary"), vmem_limit_bytes=VMEM_LIMIT),
        name="rnn",
    )(main3, main3, conv_w, conv_b, w_r, b_r, w_i, b_i, a_param)


def _cum_body(flog_ref, fb_ref, qaugt_ref, kaug_ref):
    s = flog_ref.shape[0]
    z = flog_ref[...] + fb_ref[...]
    c = -_softplus(-z) * LOG2E
    row = lax.broadcasted_iota(jnp.int32, c.shape, 0)
    d = 1
    while d < s:
        c = c + jnp.where(row >= d, pltpu.roll(c, d, 0), 0.0)
        d *= 2
    hi = c.astype(BF16).astype(F32)
    mid = (c - hi).astype(BF16).astype(F32)
    lo = (c - hi - mid).astype(BF16).astype(F32)
    slot = lax.broadcasted_iota(jnp.int32, c.shape, 1) % AUG_GROUP
    part = jnp.where(slot % AUG_PARTS == 0, hi, jnp.where(slot % AUG_PARTS == 1, mid, lo))
    used = slot < AUG_SLOTS
    qside = slot < AUG_PARTS
    qaug = jnp.where(used, jnp.where(qside, part, 1.0), 0.0)
    kaug = jnp.where(used, jnp.where(qside, 1.0, -part), 0.0)
    qaugt_ref[...] = qaug.T.astype(BF16)
    kaug_ref[...] = kaug.astype(BF16)


def _cum(flog3, fb):
    b, s, _ = flog3.shape
    return pl.pallas_call(
        _cum_body,
        grid=(b,),
        in_specs=[
            pl.BlockSpec((None, s, LANES), lambda i: (i, 0, 0)),
            pl.BlockSpec((1, LANES), lambda i: (0, 0)),
        ],
        out_specs=[
            pl.BlockSpec((None, LANES, s), lambda i: (i, 0, 0)),
            pl.BlockSpec((None, s, LANES), lambda i: (i, 0, 0)),
        ],
        out_shape=[
            jax.ShapeDtypeStruct((b, LANES, s), BF16),
            jax.ShapeDtypeStruct((b, s, LANES), BF16),
        ],
        compiler_params=pltpu.CompilerParams(dimension_semantics=("arbitrary",)),
        name="fox_cumsum",
    )(flog3, fb)


def _attend(work, q_tile, k_tile, vt_tile, finish, sc_scr, p_scr, acc_scr):
    maps = range(2)
    tq = sc_scr.shape[-1]
    items = [(i, j, bias, lo, n == 0, n == len(tiles) - 1)
             for i, tiles in enumerate(work) for n, (j, bias, lo) in enumerate(tiles)]
    assert all(lo == 0 for _, _, _, lo, first, _ in items if first)

    def scores(item, slot):
        i, j, bias, lo, _, _ = item
        tops = []
        for w in maps:
            sc = jnp.dot(k_tile(j, w), q_tile(i, w)[:, lo:], preferred_element_type=F32)
            if bias is not None:
                sc = sc + bias[:, lo:]
            sc_scr[slot, w, :, lo:] = sc
            tops.append(jnp.max(sc, axis=0, keepdims=True))
        return tuple(tops)

    def values(item, slot, alphas):
        i, j, _, lo, first, last = item
        for w in maps:
            pv = jnp.dot(vt_tile(j, w), p_scr[slot, w, :, lo:], preferred_element_type=F32)
            acc_scr[w, :, lo:] = pv if first else alphas[w] * acc_scr[w, :, lo:] + pv
        if last:
            finish(i, acc_scr[0], acc_scr[1])

    ms = alphas = None
    tops = scores(items[0], 0)
    for g, item in enumerate(items):
        slot = g % 2
        if g > 0:
            values(items[g - 1], 1 - slot, alphas)
        next_tops = scores(items[g + 1], 1 - slot) if g + 1 < len(items) else None
        lo, first = item[3], item[4]
        if first:
            ms = (jnp.full((1, tq), NEG, F32),) * 2
        new_ms, alphas = [], []
        for w in maps:
            m_old = ms[w][:, lo:]
            m_new = jnp.maximum(m_old, tops[w])
            alphas.append(jnp.exp2(m_old - m_new))
            p_scr[slot, w, :, lo:] = jnp.exp2(sc_scr[slot, w, :, lo:] - m_new).astype(BF16)
            new_ms.append(m_new if lo == 0 else jnp.concatenate([ms[w][:, :lo], m_new], axis=1))
        ms, tops = new_ms, next_tops
    values(items[-1], (len(items) - 1) % 2, alphas)


def _diff_body(rel_ref, qt_ref, k_ref, vt_ref, lq1_ref, lk1_ref, lq2_ref, lk2_ref, sg_ref, o_ref,
               bias_scr, qa_scr, qb_scr, vx_scr, sc_scr, p_scr, acc_scr, *, lam_init):
    s = k_ref.shape[0]
    head = pl.program_id(0)

    @pl.when(pl.program_id(1) == 0)
    def _():
        ki = lax.broadcasted_iota(jnp.int32, (TK, TQ), 0)
        qi = lax.broadcasted_iota(jnp.int32, (TK, TQ), 1)
        far = rel_ref[N_BUCKETS - 1, head]
        for idx in range(3):
            dist = qi - ki + (1 - idx) * TK
            n = jnp.maximum(dist, 0)
            nf = jnp.maximum(n, 1).astype(F32)
            large = MAX_EXACT + (jnp.log(nf / MAX_EXACT) / math.log(MAX_DISTANCE / MAX_EXACT)
                                 * (N_BUCKETS - MAX_EXACT)).astype(jnp.int32)
            large = jnp.minimum(large, N_BUCKETS - 1)
            bucket = jnp.where(n < MAX_EXACT, n, large)
            bias = jnp.zeros((TK, TQ), F32)
            for bk in range(N_BUCKETS):
                bias = jnp.where(bucket == bk, rel_ref[bk, head], bias)
            bias_scr[idx] = jnp.where(dist >= 0, (bias - far) * LOG2E, NEG)

    rowi = lax.broadcasted_iota(jnp.int32, qt_ref.shape, 0)
    qt = qt_ref[...].astype(F32)
    qa_scr[...] = jnp.where(rowi < DIFF_HEAD_DIM, qt, 0.0).astype(BF16)
    qb_scr[...] = jnp.where(rowi < DIFF_HEAD_DIM, 0.0, qt).astype(BF16)
    vx_scr[0:LANES, :] = vt_ref[...]
    vx_scr[LANES:, :] = jnp.ones((BF16_ROWS, s), BF16)

    lam = (jnp.exp(jnp.sum(lq1_ref[...] * lk1_ref[...], axis=-1, keepdims=True))
           - jnp.exp(jnp.sum(lq2_ref[...] * lk2_ref[...], axis=-1, keepdims=True)) + lam_init)

    k_tile = lambda j, w: k_ref[pl.ds(j * TK, TK), :]
    vt_tile = lambda j, w: vx_scr[:, pl.ds(j * TK, TK)]
    q_tile = lambda i, w: (qa_scr, qb_scr)[w][:, i * TQ:(i + 1) * TQ]
    work = []
    for i in range(s // TQ):
        j0 = i * (TQ // TK)
        near = [(j0 - 1 + idx, bias_scr.at[idx], max(idx - 1, 0) * TK) for idx in range(3) if j0 - 1 + idx >= 0]
        work.append([(j, None, 0) for j in range(j0 - 1)] + near)

    def finish(i, acca, accb):
        o = (acca[:LANES] / acca[LANES:LANES + 1] - lam * (accb[:LANES] / accb[LANES:LANES + 1]))
        o = o * lax.rsqrt(jnp.mean(o * o, axis=0, keepdims=True) + EPS)
        o_ref[i * TQ:(i + 1) * TQ, :] = (o.T * sg_ref[...] * (1.0 - lam_init)).astype(o_ref.dtype)

    _attend(work, q_tile, k_tile, vt_tile, finish, sc_scr, p_scr, acc_scr)


def _diff_attn(main3, maint, rel_bias, lq1, lk1, lq2, lk2, subln_g, lam_init):
    b, s, _ = main3.shape
    small = lambda w: pl.BlockSpec((1, w), lambda h, i: (0, 0))
    trow = lambda base: pl.BlockSpec((None, LANES, s), lambda h, i: (i, base // LANES + h, 0))
    return pl.pallas_call(
        functools.partial(_diff_body, lam_init=lam_init),
        grid=(DIFF_HEADS, b),
        in_specs=[
            pl.BlockSpec(memory_space=pltpu.SMEM),
            trow(ROW_DQ),
            pl.BlockSpec((None, s, LANES), lambda h, i: (i, 0, COL_DK // LANES + h)),
            trow(ROW_DV),
            small(DIFF_HEAD_DIM), small(DIFF_HEAD_DIM), small(DIFF_HEAD_DIM), small(DIFF_HEAD_DIM),
            small(2 * DIFF_HEAD_DIM),
        ],
        out_specs=pl.BlockSpec((None, s, LANES), lambda h, i: (i, 0, h)),
        out_shape=jax.ShapeDtypeStruct((b, s, DIFF_WIDTH), BF16),
        scratch_shapes=[
            pltpu.VMEM((3, TK, TQ), F32),
            pltpu.VMEM((LANES, s), BF16),
            pltpu.VMEM((LANES, s), BF16),
            pltpu.VMEM((LANES + BF16_ROWS, s), BF16),
            pltpu.VMEM((2, 2, TK, TQ), F32),
            pltpu.VMEM((2, 2, TK, TQ), BF16),
            pltpu.VMEM((2, LANES + BF16_ROWS, TQ), F32),
        ],
        compiler_params=pltpu.CompilerParams(
            dimension_semantics=("arbitrary", "arbitrary"), vmem_limit_bytes=VMEM_LIMIT),
        name="diff_attn",
    )(rel_bias, maint, main3, maint, lq1, lk1, lq2, lk2, subln_g)


def _fox_body(qt_ref, k_ref, vt_ref, qaugt_ref, kaug_ref, o_ref,
              mask_scr, qa_scr, qb_scr, ka_scr, kb_scr, va_scr, vb_scr, sc_scr, p_scr, acc_scr):
    s = k_ref.shape[0]
    pair = pl.program_id(0)

    @pl.when(pl.program_id(1) == 0)
    def _():
        ki = lax.broadcasted_iota(jnp.int32, (TK, TQ), 0)
        qi = lax.broadcasted_iota(jnp.int32, (TK, TQ), 1)
        for idx in range(TQ // TK):
            mask_scr[idx] = jnp.where(qi >= ki + idx * TK, 0.0, NEG)

    half = FOX_HEAD_DIM
    lo_a, lo_b = half + AUG_GROUP * pair, AUG_GROUP * pair

    def split(x, aug, idx):
        is_a = idx < half
        aug_a = jnp.where((idx >= lo_a) & (idx < lo_a + AUG_SLOTS), aug, 0.0)
        aug_b = jnp.where((idx >= lo_b) & (idx < lo_b + AUG_SLOTS), aug, 0.0)
        return jnp.where(is_a, x, aug_a).astype(BF16), jnp.where(is_a, aug_b, x).astype(BF16)

    rowi = lax.broadcasted_iota(jnp.int32, qt_ref.shape, 0)
    lanei = lax.broadcasted_iota(jnp.int32, k_ref.shape, 1)
    qa_scr[...], qb_scr[...] = split(qt_ref[...].astype(F32), qaugt_ref[...].astype(F32), rowi)
    ka_scr[...], kb_scr[...] = split(k_ref[...].astype(F32), kaug_ref[...].astype(F32), lanei)
    vt = vt_ref[...].astype(F32)
    va_scr[...] = jnp.where(rowi < half, vt, 1.0).astype(BF16)
    vb_scr[...] = jnp.where(rowi < half, 1.0, vt).astype(BF16)

    k_tile = lambda j, w: (ka_scr, kb_scr)[w][pl.ds(j * TK, TK), :]
    vt_tile = lambda j, w: (va_scr, vb_scr)[w][:, pl.ds(j * TK, TK)]
    rows = lax.broadcasted_iota(jnp.int32, (LANES, TQ), 0)
    q_tile = lambda i, w: (qa_scr, qb_scr)[w][:, i * TQ:(i + 1) * TQ]
    work = []
    for i in range(s // TQ):
        j0 = i * (TQ // TK)
        work.append([(j, None, 0) for j in range(j0)]
                    + [(j0 + idx, mask_scr.at[idx], idx * TK) for idx in range(TQ // TK)])

    def finish(i, acca, accb):
        o = jnp.where(rows < half, acca / acca[half:half + 1], accb / accb[0:1])
        o_ref[i * TQ:(i + 1) * TQ, :] = o.T.astype(o_ref.dtype)

    _attend(work, q_tile, k_tile, vt_tile, finish, sc_scr, p_scr, acc_scr)


def _fox_attn(main3, maint, qaugt, kaug):
    b, s, _ = main3.shape
    trow = lambda base: pl.BlockSpec((None, LANES, s), lambda p, i: (i, base // LANES + p, 0))
    tsc = pltpu.VMEM((LANES, s), BF16)
    return pl.pallas_call(
        _fox_body,
        grid=(FOX_HEADS // 2, b),
        in_specs=[
            trow(ROW_FQ),
            pl.BlockSpec((None, s, LANES), lambda p, i: (i, 0, COL_FK // LANES + p)),
            trow(ROW_FV),
            pl.BlockSpec((None, LANES, s), lambda p, i: (i, 0, 0)),
            pl.BlockSpec((None, s, LANES), lambda p, i: (i, 0, 0)),
        ],
        out_specs=pl.BlockSpec((None, s, LANES), lambda p, i: (i, 0, p)),
        out_shape=jax.ShapeDtypeStruct((b, s, FOX_WIDTH), BF16),
        scratch_shapes=[
            pltpu.VMEM((TQ // TK, TK, TQ), F32),
            tsc, tsc, pltpu.VMEM((s, LANES), BF16), pltpu.VMEM((s, LANES), BF16), tsc, tsc,
            pltpu.VMEM((2, 2, TK, TQ), F32),
            pltpu.VMEM((2, 2, TK, TQ), BF16),
            pltpu.VMEM((2, LANES, TQ), F32),
        ],
        compiler_params=pltpu.CompilerParams(
            dimension_semantics=("arbitrary", "arbitrary"), vmem_limit_bytes=VMEM_LIMIT),
        name="fox_attn",
    )(maint, main3, maint, qaugt, kaug)


def _merge_body(x_ref, g0_ref, g1_ref, g2_ref, gb_ref, yr_ref, yd_ref, yf_ref, wr_ref, wd_ref, wf_ref, wo_ref,
                o_ref):
    def branch(g_ref, k, y_ref, w_ref):
        gate = jax.nn.sigmoid(g_ref[...].astype(F32) + gb_ref[k:k + 1, :])
        return gate * jnp.dot(y_ref[...], w_ref[...], preferred_element_type=F32)

    m = branch(g0_ref, 0, yr_ref, wr_ref) + branch(g1_ref, 1, yd_ref, wd_ref) + branch(g2_ref, 2, yf_ref, wf_ref)
    o_ref[...] = x_ref[...] + jnp.dot(m.astype(BF16), wo_ref[...], preferred_element_type=F32)


def _merge(x2, main2, gate_b, y_rnn, y_diff, y_fox, w_r, w_d, w_f, w_o):
    t, d = x2.shape
    tm = min(TM_MERGE, t)
    rows = lambda w, c=0: pl.BlockSpec((tm, w), lambda i: (i, c))
    whole = lambda a: pl.BlockSpec(a.shape, lambda i: (0, 0))
    g0 = COL_GATES // d
    return pl.pallas_call(
        _merge_body,
        grid=(t // tm,),
        in_specs=[
            rows(d), rows(d, g0), rows(d, g0 + 1), rows(d, g0 + 2), whole(gate_b),
            rows(D_RNN), rows(DIFF_WIDTH), rows(FOX_WIDTH),
            whole(w_r), whole(w_d), whole(w_f), whole(w_o),
        ],
        out_specs=rows(d),
        out_shape=jax.ShapeDtypeStruct((t, d), F32),
        compiler_params=pltpu.CompilerParams(dimension_semantics=("arbitrary",), vmem_limit_bytes=VMEM_LIMIT),
        name="merge",
    )(x2, main2, main2, main2, gate_b, y_rnn, y_diff, y_fox, w_r, w_d, w_f, w_o)


def _ffn_body(x_ref, g_ref, wu_ref, cw_ref, cb_ref, wd_ref, gout_ref, o_ref,
              h_scr, acc_scr, act_scr, carry_scr, *, n_tiles, tiles_per_seq, out_norm):
    i = pl.program_id(0)
    tm = x_ref.shape[0]
    part = lambda c, off=0: pl.ds(off + FF_SPLIT[c], FF_SPLIT[c + 1] - FF_SPLIT[c])

    @pl.when(i == 0)
    def _():
        act_scr[...] = jnp.zeros(act_scr.shape, BF16)
        acc_scr[...] = jnp.zeros(acc_scr.shape, F32)

    @pl.when((jnp.minimum(i, n_tiles - 1) % tiles_per_seq) == 0)
    def _():
        carry_scr[...] = jnp.zeros(carry_scr.shape, F32)

    h_scr[...] = _rms(x_ref[...], g_ref[...]).astype(BF16)

    def up(c, off):
        return jnp.dot(h_scr[...], wu_ref[:, part(c, off)], preferred_element_type=F32)

    def activation(us, c):
        out = []
        for u, off in zip(us, (0, D_FF)):
            cols = part(c, off)
            before = carry_scr[:, cols]
            carry_scr[:, cols] = u[tm - SUBLANES:, :]
            out.append(cb_ref[:, cols] + _shift_rows(u, before, 2) * cw_ref[0:1, cols]
                       + _shift_rows(u, before, 1) * cw_ref[1:2, cols] + u * cw_ref[2:3, cols])
        return (jax.nn.gelu(out[0]) * out[1]).astype(BF16)

    down = lambda act, c: jnp.dot(act, wd_ref[part(c), :], preferred_element_type=F32)

    ug0 = up(0, 0)
    done = acc_scr[...] + down(act_scr[...], 1)
    o_ref[...] = _rms(done, gout_ref[...]) if out_norm else done
    uv0 = up(0, D_FF)
    act0 = activation((ug0, uv0), 0)
    ug1 = up(1, 0)
    acc_scr[...] = x_ref[...] + down(act0, 0)
    uv1 = up(1, D_FF)
    act_scr[...] = activation((ug1, uv1), 1)


def _ffn(x2, g, w_up, conv_w, conv_b, w_down, g_out, seq_len, out_norm):
    t, d = x2.shape
    tm = min(TM_FFN, seq_len)
    n_tiles = t // tm
    return pl.pallas_call(
        functools.partial(_ffn_body, n_tiles=n_tiles, tiles_per_seq=seq_len // tm, out_norm=out_norm),
        grid=(n_tiles + 1,),
        in_specs=[
            pl.BlockSpec((tm, d), lambda i: (jnp.minimum(i, n_tiles - 1), 0)),
            _resident(g.shape), _resident(w_up.shape), _resident(conv_w.shape), _resident(conv_b.shape),
            _resident(w_down.shape), _resident(g_out.shape),
        ],
        out_specs=pl.BlockSpec((tm, d), lambda i: (jnp.maximum(i - 1, 0), 0)),
        out_shape=jax.ShapeDtypeStruct((t, d), F32),
        scratch_shapes=[
            pltpu.VMEM((tm, d), BF16),
            pltpu.VMEM((tm, d), F32),
            pltpu.VMEM((tm, FF_SPLIT[2] - FF_SPLIT[1]), BF16),
            pltpu.VMEM((SUBLANES, 2 * D_FF), F32),
        ],
        compiler_params=pltpu.CompilerParams(dimension_semantics=("arbitrary",), vmem_limit_bytes=VMEM_LIMIT),
        name="ffn",
    )(x2, g, w_up, conv_w, conv_b, w_down, g_out)


def _aug_head_of_lane():
    heads = []
    for lane in range(LANES):
        group, slot = (lane % (LANES // 2)) // AUG_GROUP, lane % AUG_GROUP
        if group < FOX_HEADS // 2 and slot < AUG_SLOTS:
            heads.append(2 * group + (1 if lane < LANES // 2 else 0))
        else:
            heads.append(-1)
    return heads


def _prep_w_in(w_in, fox_b_f):
    colscale = jnp.ones((w_in.shape[-1],), F32)
    colscale = colscale.at[W_DQ:W_DQ + DIFF_QK].set(DIFF_HEAD_DIM ** -0.5 * LOG2E)
    colscale = colscale.at[W_FQ:W_FQ + FOX_WIDTH].set(FOX_HEAD_DIM ** -0.5 * LOG2E)
    wb = (w_in * colscale).astype(BF16)
    heads = _aug_head_of_lane()
    idx = jnp.array([max(h, 0) for h in heads], jnp.int32)
    live = jnp.array([1.0 if h >= 0 else 0.0 for h in heads], F32)
    out = []
    for l in range(w_in.shape[0]):
        seg = lambda a, n, l=l: wb[l, :, a:a + n]
        w_row = jnp.concatenate([seg(W_XRNN, D_RNN), seg(W_GRNN, D_RNN), seg(W_DK, DIFF_QK), seg(W_FK, FOX_WIDTH),
                                 seg(W_GATES, N_BRANCH * D_MODEL)], axis=1)
        w_t = jnp.concatenate([seg(W_DQ, DIFF_QK), seg(W_FQ, FOX_WIDTH), seg(W_DV, DIFF_WIDTH),
                               seg(W_FV, FOX_WIDTH)], axis=1).T
        w_f = (w_in[l, :, W_FLOG:W_FLOG + FOX_HEADS][:, idx] * live).astype(BF16)
        fb = (fox_b_f[l][idx] * live).reshape(1, LANES)
        out.append((w_row, w_t, w_f, fb))
    return out


def kernel(x, norm1_g, w_in, rnn_conv_w, rnn_conv_b, rg_w_r, rg_b_r, rg_w_i, rg_b_i, rg_a,
           diff_lq1, diff_lk1, diff_lq2, diff_lk2, diff_subln_g, rel_bias, fox_b_f, gate_b,
           w_br_rnn, w_br_diff, w_br_fox, w_out, norm2_g, ffn_up, ffn_conv_w, ffn_conv_b,
           ffn_down, final_g):
    bsz, s_len, d = x.shape
    t = bsz * s_len
    x2 = x.reshape(t, d)
    row = lambda v: v.reshape(1, -1)
    w_in_parts = _prep_w_in(w_in, fox_b_f)
    rg_w_r, rg_w_i, w_br_rnn, w_br_diff, w_br_fox, w_out, ffn_up, ffn_down = (
        w.astype(BF16) for w in (rg_w_r, rg_w_i, w_br_rnn, w_br_diff, w_br_fox, w_out, ffn_up, ffn_down))
    for l in range(DEPTH):
        w_row, w_t, w_f, fb = w_in_parts[l]
        main2, maint, flog = _inproj(x2, row(norm1_g[l]), w_row, w_t, w_f, bsz, s_len)
        main3 = main2.reshape(bsz, s_len, N_ROWOUT)

        y_rnn = _rnn(main3, rnn_conv_w[l], row(rnn_conv_b[l]), rg_w_r[l], row(rg_b_r[l]),
                     rg_w_i[l], row(rg_b_i[l]), row(rg_a[l]))

        lam_init = 0.8 - 0.6 * math.exp(-0.3 * l)
        y_diff = _diff_attn(main3, maint, rel_bias, row(diff_lq1[l]), row(diff_lk1[l]), row(diff_lq2[l]),
                            row(diff_lk2[l]), row(diff_subln_g[l]), lam_init)

        qaugt, kaug = _cum(flog.reshape(bsz, s_len, LANES), fb)
        y_fox = _fox_attn(main3, maint, qaugt, kaug)

        x2 = _merge(x2, main2, gate_b[l], y_rnn.reshape(t, D_RNN), y_diff.reshape(t, DIFF_WIDTH),
                    y_fox.reshape(t, FOX_WIDTH), w_br_rnn[l], w_br_diff[l], w_br_fox[l], w_out[l])

        x2 = _ffn(x2, row(norm2_g[l]), ffn_up[l], ffn_conv_w[l], row(ffn_conv_b[l]),
                  ffn_down[l], row(final_g), s_len, out_norm=(l == DEPTH - 1))
    return x2.reshape(bsz, s_len, d)
```

```python
import functools
import math

import jax
import jax.numpy as jnp
from jax import lax
from jax.experimental import pallas as pl
from jax.experimental.pallas import tpu as pltpu

F32 = jnp.float32
BF16 = jnp.bfloat16

D_MODEL = 1024
DEPTH = 2
D_RNN = D_MODEL
RNN_BLOCKS = 8
RNN_BLOCK_W = D_RNN // RNN_BLOCKS
RG_LRU_C = 8.0
DIFF_HEADS = 4
DIFF_HEAD_DIM = 64
DIFF_QK = DIFF_HEADS * 2 * DIFF_HEAD_DIM
DIFF_WIDTH = DIFF_HEADS * 2 * DIFF_HEAD_DIM
FOX_HEADS = 8
FOX_HEAD_DIM = 64
FOX_WIDTH = FOX_HEADS * FOX_HEAD_DIM
N_BUCKETS = 32
MAX_EXACT = N_BUCKETS // 2
MAX_DISTANCE = 128
D_FF = ((8 * D_MODEL // 3 + 127) // 128) * 128
N_BRANCH = 3
EPS = 1e-6
LOG2E = 1.4426950408889634
NEG = -1e30

LANES = 128
SUBLANES = 8
BF16_ROWS = 16

W_XRNN = 0
W_GRNN = W_XRNN + D_RNN
W_DQ = W_GRNN + D_RNN
W_DK = W_DQ + DIFF_QK
W_DV = W_DK + DIFF_QK
W_FQ = W_DV + DIFF_WIDTH
W_FK = W_FQ + FOX_WIDTH
W_FV = W_FK + FOX_WIDTH
W_FLOG = W_FV + FOX_WIDTH
W_GATES = W_FLOG + FOX_HEADS

COL_XRNN = 0
COL_GRNN = COL_XRNN + D_RNN
COL_DK = COL_GRNN + D_RNN
COL_FK = COL_DK + DIFF_QK
COL_GATES = COL_FK + FOX_WIDTH
N_ROWOUT = COL_GATES + N_BRANCH * D_MODEL
ROW_DQ = 0
ROW_FQ = ROW_DQ + DIFF_QK
ROW_DV = ROW_FQ + FOX_WIDTH
ROW_FV = ROW_DV + DIFF_WIDTH
N_TOUT = ROW_FV + FOX_WIDTH

TQ = 512
TK = 256
TM_PROJ = 512
TN_PROJ = 1024
TM_MERGE = 512
TM_FFN = 512
MXU_TILE = 256
FF_SPLIT = (0, (D_FF // MXU_TILE + 1) // 2 * MXU_TILE, D_FF)
RNN_HALF = D_RNN // 2
VMEM_LIMIT = 56 * 1024 * 1024

AUG_GROUP = 8
AUG_SLOTS = 6
AUG_PARTS = 3


def _softplus(z):
    return jnp.maximum(z, 0.0) + jnp.log1p(jnp.exp(-jnp.abs(z)))


def _rms(x, g):
    return x * lax.rsqrt(jnp.mean(x * x, axis=-1, keepdims=True) + EPS) * g


def _sigmoid(z):
    return 0.5 * jnp.tanh(0.5 * z) + 0.5


def _shift_rows(u, before, k):
    rows, width = u.shape
    groups = rows // SUBLANES
    rot = pltpu.roll(u.reshape(groups, SUBLANES, width), k, 1)
    prev = jnp.concatenate([pltpu.roll(before, k, 0)[None], rot[:-1]], axis=0)
    sub = lax.broadcasted_iota(jnp.int32, rot.shape, 1)
    return jnp.where(sub < k, prev, rot).reshape(rows, width)


def _inproj_body(x_ref, g_ref, w_ref, wt_ref, wf_ref, out_ref, outt_ref, flog_ref, h_scr):
    h_scr[...] = _rms(x_ref[...], g_ref[...]).astype(BF16)
    flog_ref[...] = jnp.dot(h_scr[...], wf_ref[...], preferred_element_type=F32)
    for j in range(w_ref.shape[1] // TN_PROJ):
        cols = slice(j * TN_PROJ, (j + 1) * TN_PROJ)
        out_ref[:, cols] = jnp.dot(h_scr[...], w_ref[:, cols], preferred_element_type=F32).astype(out_ref.dtype)
    for j in range(wt_ref.shape[0] // TN_PROJ):
        rows = slice(j * TN_PROJ, (j + 1) * TN_PROJ)
        outt_ref[rows, :] = lax.dot_general(wt_ref[rows, :], h_scr[...], (((1,), (1,)), ((), ())),
                                            preferred_element_type=F32).astype(outt_ref.dtype)


def _resident(shape):
    return pl.BlockSpec(shape, lambda *_: (0,) * len(shape), pipeline_mode=pl.Buffered(1))


def _inproj(x2, g, w_row, w_t, w_f, bsz, s_len):
    t, d = x2.shape
    tm = min(TM_PROJ, s_len)
    tps = s_len // tm
    return pl.pallas_call(
        _inproj_body,
        grid=(t // tm,),
        in_specs=[
            pl.BlockSpec((tm, d), lambda i: (i, 0)),
            _resident(g.shape), _resident(w_row.shape), _resident(w_t.shape), _resident(w_f.shape),
        ],
        out_specs=[
            pl.BlockSpec((tm, N_ROWOUT), lambda i: (i, 0)),
            pl.BlockSpec((None, N_TOUT, tm), lambda i: (i // tps, 0, i % tps)),
            pl.BlockSpec((tm, LANES), lambda i: (i, 0)),
        ],
        out_shape=[
            jax.ShapeDtypeStruct((t, N_ROWOUT), BF16),
            jax.ShapeDtypeStruct((bsz, N_TOUT, s_len), BF16),
            jax.ShapeDtypeStruct((t, LANES), F32),
        ],
        scratch_shapes=[pltpu.VMEM((tm, d), BF16)],
        compiler_params=pltpu.CompilerParams(dimension_semantics=("arbitrary",), vmem_limit_bytes=VMEM_LIMIT),
        name="inproj",
    )(x2, g, w_row, w_t, w_f)


def _rnn_body(xr_ref, gr_ref, cw_ref, cb_ref, wr_ref, br_ref, wi_ref, bi_ref, ap_ref, y_ref, a_scr, u_scr):
    s, width = xr_ref.shape
    nblk = width // RNN_BLOCK_W
    half = pl.program_id(1)
    zeros = jnp.zeros((SUBLANES, RNN_BLOCK_W), F32)
    for n in range(nblk):
        sl = slice(n * RNN_BLOCK_W, (n + 1) * RNN_BLOCK_W)
        x = xr_ref[:, sl].astype(F32)
        xc = (cb_ref[:, sl] + _shift_rows(x, zeros, 3) * cw_ref[0:1, sl] + _shift_rows(x, zeros, 2) * cw_ref[1:2, sl]
              + _shift_rows(x, zeros, 1) * cw_ref[2:3, sl] + x * cw_ref[3:4, sl])
        xb = xc.astype(BF16)
        blk = half * nblk + n
        zr = jnp.dot(xb, wr_ref[blk], preferred_element_type=F32) + br_ref[:, sl]
        gi = _sigmoid(jnp.dot(xb, wi_ref[blk], preferred_element_type=F32) + bi_ref[:, sl])
        k = (-0.5 * RG_LRU_C) * _softplus(-ap_ref[:, sl])
        a = jnp.exp(k * jnp.tanh(0.5 * zr) + k)
        a_scr[:, sl] = a
        y = 1.0 - a * a
        u_scr[:, sl] = jnp.where(y > 0.0, y * lax.rsqrt(y), 0.0) * (gi * xc)

    row8 = lax.broadcasted_iota(jnp.int32, (SUBLANES, width), 0)

    def group(k, hprev):
        off = pl.multiple_of(k * SUBLANES, SUBLANES)
        a = a_scr[pl.ds(off, SUBLANES), :]
        u = u_scr[pl.ds(off, SUBLANES), :]
        for d in (1, 2, 4):
            a_sh = jnp.where(row8 >= d, pltpu.roll(a, d, 0), 1.0)
            u_sh = jnp.where(row8 >= d, pltpu.roll(u, d, 0), 0.0)
            u = a * u_sh + u
            a = a * a_sh
        h = a * hprev + u
        u_scr[pl.ds(off, SUBLANES), :] = h
        return h[SUBLANES - 1:SUBLANES, :]

    lax.fori_loop(0, s // SUBLANES, group, jnp.zeros((1, width), F32))

    for n in range(nblk):
        sl = slice(n * RNN_BLOCK_W, (n + 1) * RNN_BLOCK_W)
        y_ref[:, sl] = (jax.nn.gelu(gr_ref[:, sl].astype(F32)) * u_scr[:, sl]).astype(y_ref.dtype)


def _rnn(main3, conv_w, conv_b, w_r, b_r, w_i, b_i, a_param):
    b, s, _ = main3.shape
    w = RNN_HALF
    nh = D_RNN // w
    vec = lambda k: pl.BlockSpec((k, w), lambda i, j: (0, j))
    full3 = pl.BlockSpec((RNN_BLOCKS, RNN_BLOCK_W, RNN_BLOCK_W), lambda i, j: (0, 0, 0))
    return pl.pallas_call(
        _rnn_body,
        grid=(b, nh),
        in_specs=[
            pl.BlockSpec((None, s, w), lambda i, j: (i, 0, COL_XRNN // w + j)),
            pl.BlockSpec((None, s, w), lambda i, j: (i, 0, COL_GRNN // w + j)),
            vec(4), vec(1), full3, vec(1), full3, vec(1), vec(1),
        ],
        out_specs=pl.BlockSpec((None, s, w), lambda i, j: (i, 0, j)),
        out_shape=jax.ShapeDtypeStruct((b, s, D_RNN), BF16),
        scratch_shapes=[pltpu.VMEM((s, w), F32), pltpu.VMEM((s, w), F32)],
        compiler_params=pltpu.CompilerParams(
            dimension_semantics=("arbitrary", "arbitrary"), vmem_limit_bytes=VMEM_LIMIT),
        name="rnn",
    )(main3, main3, conv_w, conv_b, w_r, b_r, w_i, b_i, a_param)


def _cum_body(flog_ref, fb_ref, qaugt_ref, kaug_ref):
    s = flog_ref.shape[0]
    z = flog_ref[...] + fb_ref[...]
    c = -_softplus(-z) * LOG2E
    row = lax.broadcasted_iota(jnp.int32, c.shape, 0)
    d = 1
    while d < s:
        c = c + jnp.where(row >= d, pltpu.roll(c, d, 0), 0.0)
        d *= 2
    hi = c.astype(BF16).astype(F32)
    mid = (c - hi).astype(BF16).astype(F32)
    lo = (c - hi - mid).astype(BF16).astype(F32)
    slot = lax.broadcasted_iota(jnp.int32, c.shape, 1) % AUG_GROUP
    part = jnp.where(slot % AUG_PARTS == 0, hi, jnp.where(slot % AUG_PARTS == 1, mid, lo))
    used = slot < AUG_SLOTS
    qside = slot < AUG_PARTS
    qaug = jnp.where(used, jnp.where(qside, part, 1.0), 0.0)
    kaug = jnp.where(used, jnp.where(qside, 1.0, -part), 0.0)
    qaugt_ref[...] = qaug.T.astype(BF16)
    kaug_ref[...] = kaug.astype(BF16)


def _cum(flog3, fb):
    b, s, _ = flog3.shape
    return pl.pallas_call(
        _cum_body,
        grid=(b,),
        in_specs=[
            pl.BlockSpec((None, s, LANES), lambda i: (i, 0, 0)),
            pl.BlockSpec((1, LANES), lambda i: (0, 0)),
        ],
        out_specs=[
            pl.BlockSpec((None, LANES, s), lambda i: (i, 0, 0)),
            pl.BlockSpec((None, s, LANES), lambda i: (i, 0, 0)),
        ],
        out_shape=[
            jax.ShapeDtypeStruct((b, LANES, s), BF16),
            jax.ShapeDtypeStruct((b, s, LANES), BF16),
        ],
        compiler_params=pltpu.CompilerParams(dimension_semantics=("arbitrary",)),
        name="fox_cumsum",
    )(flog3, fb)


def _attend(work, q_tile, k_tile, vt_tile, finish, sc_scr, p_scr, acc_scr, dyn0=0):
    maps = range(2)
    tq = sc_scr.shape[-1]
    items = [(i, j, bias, lo, n == 0, n == len(tiles) - 1)
             for i, tiles in enumerate(work) for n, (j, bias, lo) in enumerate(tiles)]
    assert all(lo == 0 for _, _, _, lo, first, _ in items if first)

    def scores(item, slot):
        i, j, bias, lo, _, _ = item
        tops = []
        for w in maps:
            sc = jnp.dot(k_tile(j, w), q_tile(i, w)[:, lo:], preferred_element_type=F32)
            if bias is not None:
                sc = sc + bias[:, lo:]
            sc_scr[slot, w, :, lo:] = sc
            tops.append(jnp.max(sc, axis=0, keepdims=True))
        return tuple(tops)

    def values(item, slot, alphas):
        i, j, _, lo, first, last = item
        for w in maps:
            pv = jnp.dot(vt_tile(j, w), p_scr[slot, w, :, lo:], preferred_element_type=F32)
            acc_scr[w, :, lo:] = pv if first else alphas[w] * acc_scr[w, :, lo:] + pv
        if last:
            finish(i, acc_scr[0], acc_scr[1])

    ms = alphas = None
    tops = scores(items[0], 0)
    for g, item in enumerate(items):
        slot = g % 2
        if g > 0:
            values(items[g - 1], 1 - slot, alphas)
        lo, first = item[3], item[4]
        if first:
            ms = (jnp.full((1, tq), NEG, F32),) * 2
        new_ms, alphas = [], []
        for w in maps:
            m_old = ms[w][:, lo:]
            m_new = jnp.maximum(m_old, tops[w])
            alphas.append(jnp.exp2(m_old - m_new))
            p_scr[slot, w, :, lo:] = jnp.exp2(sc_scr[slot + dyn0, w, :, lo:] - m_new).astype(BF16)
            new_ms.append(m_new if lo == 0 else jnp.concatenate([ms[w][:, :lo], m_new], axis=1))
        ms = new_ms
        tops = scores(items[g + 1], 1 - slot) if g + 1 < len(items) else None
    values(items[-1], (len(items) - 1) % 2, alphas)


def _diff_body(zero_ref, rel_ref, qt_ref, k_ref, vt_ref, lq1_ref, lk1_ref, lq2_ref, lk2_ref, sg_ref, o_ref,
               bias_scr, qa_scr, qb_scr, vx_scr, sc_scr, p_scr, acc_scr, *, lam_init):
    s = k_ref.shape[0]
    head = pl.program_id(0)

    @pl.when(pl.program_id(1) == 0)
    def _():
        ki = lax.broadcasted_iota(jnp.int32, (TK, TQ), 0)
        qi = lax.broadcasted_iota(jnp.int32, (TK, TQ), 1)
        far = rel_ref[N_BUCKETS - 1, head]
        for idx in range(3):
            dist = qi - ki + (1 - idx) * TK
            n = jnp.maximum(dist, 0)
            nf = jnp.maximum(n, 1).astype(F32)
            large = MAX_EXACT + (jnp.log(nf / MAX_EXACT) / math.log(MAX_DISTANCE / MAX_EXACT)
                                 * (N_BUCKETS - MAX_EXACT)).astype(jnp.int32)
            large = jnp.minimum(large, N_BUCKETS - 1)
            bucket = jnp.where(n < MAX_EXACT, n, large)
            bias = jnp.zeros((TK, TQ), F32)
            for bk in range(N_BUCKETS):
                bias = jnp.where(bucket == bk, rel_ref[bk, head], bias)
            bias_scr[idx] = jnp.where(dist >= 0, (bias - far) * LOG2E, NEG)

    rowi = lax.broadcasted_iota(jnp.int32, qt_ref.shape, 0)
    qt = qt_ref[...].astype(F32)
    qa_scr[...] = jnp.where(rowi < DIFF_HEAD_DIM, qt, 0.0).astype(BF16)
    qb_scr[...] = jnp.where(rowi < DIFF_HEAD_DIM, 0.0, qt).astype(BF16)
    vx_scr[0:LANES, :] = vt_ref[...]
    vx_scr[LANES:, :] = jnp.ones((BF16_ROWS, s), BF16)

    lam = (jnp.exp(jnp.sum(lq1_ref[...] * lk1_ref[...], axis=-1, keepdims=True))
           - jnp.exp(jnp.sum(lq2_ref[...] * lk2_ref[...], axis=-1, keepdims=True)) + lam_init)

    k_tile = lambda j, w: k_ref[pl.ds(j * TK, TK), :]
    vt_tile = lambda j, w: vx_scr[:, pl.ds(j * TK, TK)]
    q_tile = lambda i, w: (qa_scr, qb_scr)[w][:, i * TQ:(i + 1) * TQ]
    work = []
    for i in range(s // TQ):
        j0 = i * (TQ // TK)
        near = [(j0 - 1 + idx, bias_scr.at[idx], max(idx - 1, 0) * TK) for idx in range(3) if j0 - 1 + idx >= 0]
        work.append([(j, None, 0) for j in range(j0 - 1)] + near)

    def finish(i, acca, accb):
        o = (acca[:LANES] / acca[LANES:LANES + 1] - lam * (accb[:LANES] / accb[LANES:LANES + 1]))
        o = o * lax.rsqrt(jnp.mean(o * o, axis=0, keepdims=True) + EPS)
        o_ref[i * TQ:(i + 1) * TQ, :] = (o.T * sg_ref[...] * (1.0 - lam_init)).astype(o_ref.dtype)

    _attend(work, q_tile, k_tile, vt_tile, finish, sc_scr, p_scr, acc_scr, dyn0=zero_ref[0])


def _diff_attn(main3, maint, rel_bias, lq1, lk1, lq2, lk2, subln_g, lam_init):
    b, s, _ = main3.shape
    small = lambda w: pl.BlockSpec((1, w), lambda h, i: (0, 0))
    trow = lambda base: pl.BlockSpec((None, LANES, s), lambda h, i: (i, base // LANES + h, 0))
    return pl.pallas_call(
        functools.partial(_diff_body, lam_init=lam_init),
        grid=(DIFF_HEADS, b),
        in_specs=[
            pl.BlockSpec(memory_space=pltpu.SMEM),
            pl.BlockSpec(memory_space=pltpu.SMEM),
            trow(ROW_DQ),
            pl.BlockSpec((None, s, LANES), lambda h, i: (i, 0, COL_DK // LANES + h)),
            trow(ROW_DV),
            small(DIFF_HEAD_DIM), small(DIFF_HEAD_DIM), small(DIFF_HEAD_DIM), small(DIFF_HEAD_DIM),
            small(2 * DIFF_HEAD_DIM),
        ],
        out_specs=pl.BlockSpec((None, s, LANES), lambda h, i: (i, 0, h)),
        out_shape=jax.ShapeDtypeStruct((b, s, DIFF_WIDTH), BF16),
        scratch_shapes=[
            pltpu.VMEM((3, TK, TQ), F32),
            pltpu.VMEM((LANES, s), BF16),
            pltpu.VMEM((LANES, s), BF16),
            pltpu.VMEM((LANES + BF16_ROWS, s), BF16),
            pltpu.VMEM((2, 2, TK, TQ), F32),
            pltpu.VMEM((2, 2, TK, TQ), BF16),
            pltpu.VMEM((2, LANES + BF16_ROWS, TQ), F32),
        ],
        compiler_params=pltpu.CompilerParams(
            dimension_semantics=("arbitrary", "arbitrary"), vmem_limit_bytes=VMEM_LIMIT),
        name="diff_attn",
    )(jnp.zeros((1,), jnp.int32), rel_bias, maint, main3, maint, lq1, lk1, lq2, lk2, subln_g)


def _fox_body(zero_ref, qt_ref, k_ref, vt_ref, qaugt_ref, kaug_ref, o_ref,
              mask_scr, qa_scr, qb_scr, ka_scr, kb_scr, va_scr, vb_scr, sc_scr, p_scr, acc_scr):
    s = k_ref.shape[0]
    pair = pl.program_id(0)

    @pl.when(pl.program_id(1) == 0)
    def _():
        ki = lax.broadcasted_iota(jnp.int32, (TK, TQ), 0)
        qi = lax.broadcasted_iota(jnp.int32, (TK, TQ), 1)
        for idx in range(TQ // TK):
            mask_scr[idx] = jnp.where(qi >= ki + idx * TK, 0.0, NEG)

    half = FOX_HEAD_DIM
    lo_a, lo_b = half + AUG_GROUP * pair, AUG_GROUP * pair

    def split(x, aug, idx):
        is_a = idx < half
        aug_a = jnp.where((idx >= lo_a) & (idx < lo_a + AUG_SLOTS), aug, 0.0)
        aug_b = jnp.where((idx >= lo_b) & (idx < lo_b + AUG_SLOTS), aug, 0.0)
        return jnp.where(is_a, x, aug_a).astype(BF16), jnp.where(is_a, aug_b, x).astype(BF16)

    rowi = lax.broadcasted_iota(jnp.int32, qt_ref.shape, 0)
    lanei = lax.broadcasted_iota(jnp.int32, k_ref.shape, 1)
    qa_scr[...], qb_scr[...] = split(qt_ref[...].astype(F32), qaugt_ref[...].astype(F32), rowi)
    ka_scr[...], kb_scr[...] = split(k_ref[...].astype(F32), kaug_ref[...].astype(F32), lanei)
    vt = vt_ref[...].astype(F32)
    va_scr[...] = jnp.where(rowi < half, vt, 1.0).astype(BF16)
    vb_scr[...] = jnp.where(rowi < half, 1.0, vt).astype(BF16)

    k_tile = lambda j, w: (ka_scr, kb_scr)[w][pl.ds(j * TK, TK), :]
    vt_tile = lambda j, w: (va_scr, vb_scr)[w][:, pl.ds(j * TK, TK)]
    rows = lax.broadcasted_iota(jnp.int32, (LANES, TQ), 0)
    q_tile = lambda i, w: (qa_scr, qb_scr)[w][:, i * TQ:(i + 1) * TQ]
    work = []
    for i in range(s // TQ):
        j0 = i * (TQ // TK)
        work.append([(j, None, 0) for j in range(j0)]
                    + [(j0 + idx, mask_scr.at[idx], idx * TK) for idx in range(TQ // TK)])

    def finish(i, acca, accb):
        o = jnp.where(rows < half, acca / acca[half:half + 1], accb / accb[0:1])
        o_ref[i * TQ:(i + 1) * TQ, :] = o.T.astype(o_ref.dtype)

    _attend(work, q_tile, k_tile, vt_tile, finish, sc_scr, p_scr, acc_scr, dyn0=zero_ref[0])


def _fox_attn(main3, maint, qaugt, kaug):
    b, s, _ = main3.shape
    trow = lambda base: pl.BlockSpec((None, LANES, s), lambda p, i: (i, base // LANES + p, 0))
    tsc = pltpu.VMEM((LANES, s), BF16)
    return pl.pallas_call(
        _fox_body,
        grid=(FOX_HEADS // 2, b),
        in_specs=[
            pl.BlockSpec(memory_space=pltpu.SMEM),
            trow(ROW_FQ),
            pl.BlockSpec((None, s, LANES), lambda p, i: (i, 0, COL_FK // LANES + p)),
            trow(ROW_FV),
            pl.BlockSpec((None, LANES, s), lambda p, i: (i, 0, 0)),
            pl.BlockSpec((None, s, LANES), lambda p, i: (i, 0, 0)),
        ],
        out_specs=pl.BlockSpec((None, s, LANES), lambda p, i: (i, 0, p)),
        out_shape=jax.ShapeDtypeStruct((b, s, FOX_WIDTH), BF16),
        scratch_shapes=[
            pltpu.VMEM((TQ // TK, TK, TQ), F32),
            tsc, tsc, pltpu.VMEM((s, LANES), BF16), pltpu.VMEM((s, LANES), BF16), tsc, tsc,
            pltpu.VMEM((2, 2, TK, TQ), F32),
            pltpu.VMEM((2, 2, TK, TQ), BF16),
            pltpu.VMEM((2, LANES, TQ), F32),
        ],
        compiler_params=pltpu.CompilerParams(
            dimension_semantics=("arbitrary", "arbitrary"), vmem_limit_bytes=VMEM_LIMIT),
        name="fox_attn",
    )(jnp.zeros((1,), jnp.int32), maint, main3, maint, qaugt, kaug)


def _merge_body(x_ref, g0_ref, g1_ref, g2_ref, gb_ref, yr_ref, yd_ref, yf_ref, wr_ref, wd_ref, wf_ref, wo_ref,
                o_ref):
    def branch(g_ref, k, y_ref, w_ref):
        gate = jax.nn.sigmoid(g_ref[...].astype(F32) + gb_ref[k:k + 1, :])
        return gate * jnp.dot(y_ref[...], w_ref[...], preferred_element_type=F32)

    m = branch(g0_ref, 0, yr_ref, wr_ref) + branch(g1_ref, 1, yd_ref, wd_ref) + branch(g2_ref, 2, yf_ref, wf_ref)
    o_ref[...] = x_ref[...] + jnp.dot(m.astype(BF16), wo_ref[...], preferred_element_type=F32)


def _merge(x2, main2, gate_b, y_rnn, y_diff, y_fox, w_r, w_d, w_f, w_o):
    t, d = x2.shape
    tm = min(TM_MERGE, t)
    rows = lambda w, c=0: pl.BlockSpec((tm, w), lambda i: (i, c))
    whole = lambda a: pl.BlockSpec(a.shape, lambda i: (0, 0))
    g0 = COL_GATES // d
    return pl.pallas_call(
        _merge_body,
        grid=(t // tm,),
        in_specs=[
            rows(d), rows(d, g0), rows(d, g0 + 1), rows(d, g0 + 2), whole(gate_b),
            rows(D_RNN), rows(DIFF_WIDTH), rows(FOX_WIDTH),
            whole(w_r), whole(w_d), whole(w_f), whole(w_o),
        ],
        out_specs=rows(d),
        out_shape=jax.ShapeDtypeStruct((t, d), F32),
        compiler_params=pltpu.CompilerParams(dimension_semantics=("arbitrary",), vmem_limit_bytes=VMEM_LIMIT),
        name="merge",
    )(x2, main2, main2, main2, gate_b, y_rnn, y_diff, y_fox, w_r, w_d, w_f, w_o)


def _ffn_body(x_ref, g_ref, wu_ref, cw_ref, cb_ref, wd_ref, gout_ref, o_ref,
              h_scr, acc_scr, act_scr, carry_scr, *, n_tiles, tiles_per_seq, out_norm):
    i = pl.program_id(0)
    tm = x_ref.shape[0]
    part = lambda c, off=0: pl.ds(off + FF_SPLIT[c], FF_SPLIT[c + 1] - FF_SPLIT[c])

    @pl.when(i == 0)
    def _():
        act_scr[...] = jnp.zeros(act_scr.shape, BF16)
        acc_scr[...] = jnp.zeros(acc_scr.shape, F32)

    @pl.when((jnp.minimum(i, n_tiles - 1) % tiles_per_seq) == 0)
    def _():
        carry_scr[...] = jnp.zeros(carry_scr.shape, F32)

    h_scr[...] = _rms(x_ref[...], g_ref[...]).astype(BF16)

    def up(c, off):
        return jnp.dot(h_scr[...], wu_ref[:, part(c, off)], preferred_element_type=F32)

    def activation(us, c):
        out = []
        for u, off in zip(us, (0, D_FF)):
            cols = part(c, off)
            before = carry_scr[:, cols]
            carry_scr[:, cols] = u[tm - SUBLANES:, :]
            out.append(cb_ref[:, cols] + _shift_rows(u, before, 2) * cw_ref[0:1, cols]
                       + _shift_rows(u, before, 1) * cw_ref[1:2, cols] + u * cw_ref[2:3, cols])
        return (jax.nn.gelu(out[0]) * out[1]).astype(BF16)

    down = lambda act, c: jnp.dot(act, wd_ref[part(c), :], preferred_element_type=F32)

    ug0 = up(0, 0)
    done = acc_scr[...] + down(act_scr[...], 1)
    o_ref[...] = _rms(done, gout_ref[...]) if out_norm else done
    uv0 = up(0, D_FF)
    act0 = activation((ug0, uv0), 0)
    ug1 = up(1, 0)
    acc_scr[...] = x_ref[...] + down(act0, 0)
    uv1 = up(1, D_FF)
    act_scr[...] = activation((ug1, uv1), 1)


def _ffn(x2, g, w_up, conv_w, conv_b, w_down, g_out, seq_len, out_norm):
    t, d = x2.shape
    tm = min(TM_FFN, seq_len)
    n_tiles = t // tm
    return pl.pallas_call(
        functools.partial(_ffn_body, n_tiles=n_tiles, tiles_per_seq=seq_len // tm, out_norm=out_norm),
        grid=(n_tiles + 1,),
        in_specs=[
            pl.BlockSpec((tm, d), lambda i: (jnp.minimum(i, n_tiles - 1), 0)),
            _resident(g.shape), _resident(w_up.shape), _resident(conv_w.shape), _resident(conv_b.shape),
            _resident(w_down.shape), _resident(g_out.shape),
        ],
        out_specs=pl.BlockSpec((tm, d), lambda i: (jnp.maximum(i - 1, 0), 0)),
        out_shape=jax.ShapeDtypeStruct((t, d), F32),
        scratch_shapes=[
            pltpu.VMEM((tm, d), BF16),
            pltpu.VMEM((tm, d), F32),
            pltpu.VMEM((tm, FF_SPLIT[2] - FF_SPLIT[1]), BF16),
            pltpu.VMEM((SUBLANES, 2 * D_FF), F32),
        ],
        compiler_params=pltpu.CompilerParams(dimension_semantics=("arbitrary",), vmem_limit_bytes=VMEM_LIMIT),
        name="ffn",
    )(x2, g, w_up, conv_w, conv_b, w_down, g_out)


def _aug_head_of_lane():
    heads = []
    for lane in range(LANES):
        group, slot = (lane % (LANES // 2)) // AUG_GROUP, lane % AUG_GROUP
        if group < FOX_HEADS // 2 and slot < AUG_SLOTS:
            heads.append(2 * group + (1 if lane < LANES // 2 else 0))
        else:
            heads.append(-1)
    return heads


def _prep_w_in(w_in, fox_b_f):
    colscale = jnp.ones((w_in.shape[-1],), F32)
    colscale = colscale.at[W_DQ:W_DQ + DIFF_QK].set(DIFF_HEAD_DIM ** -0.5 * LOG2E)
    colscale = colscale.at[W_FQ:W_FQ + FOX_WIDTH].set(FOX_HEAD_DIM ** -0.5 * LOG2E)
    wb = (w_in * colscale).astype(BF16)
    heads = _aug_head_of_lane()
    idx = jnp.array([max(h, 0) for h in heads], jnp.int32)
    live = jnp.array([1.0 if h >= 0 else 0.0 for h in heads], F32)
    out = []
    for l in range(w_in.shape[0]):
        seg = lambda a, n, l=l: wb[l, :, a:a + n]
        w_row = jnp.concatenate([seg(W_XRNN, D_RNN), seg(W_GRNN, D_RNN), seg(W_DK, DIFF_QK), seg(W_FK, FOX_WIDTH),
                                 seg(W_GATES, N_BRANCH * D_MODEL)], axis=1)
        w_t = jnp.concatenate([seg(W_DQ, DIFF_QK), seg(W_FQ, FOX_WIDTH), seg(W_DV, DIFF_WIDTH),
                               seg(W_FV, FOX_WIDTH)], axis=1).T
        w_f = (w_in[l, :, W_FLOG:W_FLOG + FOX_HEADS][:, idx] * live).astype(BF16)
        fb = (fox_b_f[l][idx] * live).reshape(1, LANES)
        out.append((w_row, w_t, w_f, fb))
    return out


def kernel(x, norm1_g, w_in, rnn_conv_w, rnn_conv_b, rg_w_r, rg_b_r, rg_w_i, rg_b_i, rg_a,
           diff_lq1, diff_lk1, diff_lq2, diff_lk2, diff_subln_g, rel_bias, fox_b_f, gate_b,
           w_br_rnn, w_br_diff, w_br_fox, w_out, norm2_g, ffn_up, ffn_conv_w, ffn_conv_b,
           ffn_down, final_g):
    bsz, s_len, d = x.shape
    t = bsz * s_len
    x2 = x.reshape(t, d)
    row = lambda v: v.reshape(1, -1)
    w_in_parts = _prep_w_in(w_in, fox_b_f)
    rg_w_r, rg_w_i, w_br_rnn, w_br_diff, w_br_fox, w_out, ffn_up, ffn_down = (
        w.astype(BF16) for w in (rg_w_r, rg_w_i, w_br_rnn, w_br_diff, w_br_fox, w_out, ffn_up, ffn_down))
    for l in range(DEPTH):
        w_row, w_t, w_f, fb = w_in_parts[l]
        main2, maint, flog = _inproj(x2, row(norm1_g[l]), w_row, w_t, w_f, bsz, s_len)
        main3 = main2.reshape(bsz, s_len, N_ROWOUT)

        y_rnn = _rnn(main3, rnn_conv_w[l], row(rnn_conv_b[l]), rg_w_r[l], row(rg_b_r[l]),
                     rg_w_i[l], row(rg_b_i[l]), row(rg_a[l]))

        lam_init = 0.8 - 0.6 * math.exp(-0.3 * l)
        y_diff = _diff_attn(main3, maint, rel_bias, row(diff_lq1[l]), row(diff_lk1[l]), row(diff_lq2[l]),
                            row(diff_lk2[l]), row(diff_subln_g[l]), lam_init)

        qaugt, kaug = _cum(flog.reshape(bsz, s_len, LANES), fb)
        y_fox = _fox_attn(main3, maint, qaugt, kaug)

        x2 = _merge(x2, main2, gate_b[l], y_rnn.reshape(t, D_RNN), y_diff.reshape(t, DIFF_WIDTH),
                    y_fox.reshape(t, FOX_WIDTH), w_br_rnn[l], w_br_diff[l], w_br_fox[l], w_out[l])

        x2 = _ffn(x2, row(norm2_g[l]), ffn_up[l], ffn_conv_w[l], row(ffn_conv_b[l]),
                  ffn_down[l], row(final_g), s_len, out_norm=(l == DEPTH - 1))
    return x2.reshape(bsz, s_len, d)
```

```python
import functools
import math

import jax
import jax.numpy as jnp
from jax import lax
from jax.experimental import pallas as pl
from jax.experimental.pallas import tpu as pltpu

F32 = jnp.float32
BF16 = jnp.bfloat16

D_MODEL = 1024
DEPTH = 2
D_RNN = D_MODEL
RNN_BLOCKS = 8
RNN_BLOCK_W = D_RNN // RNN_BLOCKS
RG_LRU_C = 8.0
DIFF_HEADS = 4
DIFF_HEAD_DIM = 64
DIFF_QK = DIFF_HEADS * 2 * DIFF_HEAD_DIM
DIFF_WIDTH = DIFF_HEADS * 2 * DIFF_HEAD_DIM
FOX_HEADS = 8
FOX_HEAD_DIM = 64
FOX_WIDTH = FOX_HEADS * FOX_HEAD_DIM
N_BUCKETS = 32
MAX_EXACT = N_BUCKETS // 2
MAX_DISTANCE = 128
D_FF = ((8 * D_MODEL // 3 + 127) // 128) * 128
N_BRANCH = 3
EPS = 1e-6
LOG2E = 1.4426950408889634
NEG = -1e30

LANES = 128
SUBLANES = 8
BF16_ROWS = 16

W_XRNN = 0
W_GRNN = W_XRNN + D_RNN
W_DQ = W_GRNN + D_RNN
W_DK = W_DQ + DIFF_QK
W_DV = W_DK + DIFF_QK
W_FQ = W_DV + DIFF_WIDTH
W_FK = W_FQ + FOX_WIDTH
W_FV = W_FK + FOX_WIDTH
W_FLOG = W_FV + FOX_WIDTH
W_GATES = W_FLOG + FOX_HEADS

COL_XRNN = 0
COL_GRNN = COL_XRNN + D_RNN
COL_DK = COL_GRNN + D_RNN
COL_FK = COL_DK + DIFF_QK
COL_GATES = COL_FK + FOX_WIDTH
N_ROWOUT = COL_GATES + N_BRANCH * D_MODEL
ROW_DQ = 0
ROW_FQ = ROW_DQ + DIFF_QK
ROW_DV = ROW_FQ + FOX_WIDTH
ROW_FV = ROW_DV + DIFF_WIDTH
N_TOUT = ROW_FV + FOX_WIDTH

TQ = 512
TK = 256
TM_PROJ = 512
TN_PROJ = 1024
TM_MERGE = 512
TM_FFN = 512
MXU_TILE = 256
FF_SPLIT = (0, (D_FF // MXU_TILE + 1) // 2 * MXU_TILE, D_FF)
RNN_HALF = D_RNN // 2
VMEM_LIMIT = 56 * 1024 * 1024

AUG_GROUP = 8
AUG_SLOTS = 6
AUG_PARTS = 3


def _softplus(z):
    return jnp.maximum(z, 0.0) + jnp.log1p(jnp.exp(-jnp.abs(z)))


def _rms(x, g):
    return x * lax.rsqrt(jnp.mean(x * x, axis=-1, keepdims=True) + EPS) * g


def _sigmoid(z):
    return 0.5 * jnp.tanh(0.5 * z) + 0.5


def _shift_rows(u, before, k):
    rows, width = u.shape
    groups = rows // SUBLANES
    rot = pltpu.roll(u.reshape(groups, SUBLANES, width), k, 1)
    prev = jnp.concatenate([pltpu.roll(before, k, 0)[None], rot[:-1]], axis=0)
    sub = lax.broadcasted_iota(jnp.int32, rot.shape, 1)
    return jnp.where(sub < k, prev, rot).reshape(rows, width)


def _forget_operands(f_logit, fb, carry_scr, qaugt_ref, kaug_ref):
    tm = f_logit.shape[0]
    c = -_softplus(-(f_logit + fb)) * LOG2E
    row = lax.broadcasted_iota(jnp.int32, c.shape, 0)
    d = 1
    while d < tm:
        c = c + jnp.where(row >= d, pltpu.roll(c, d, 0), 0.0)
        d *= 2
    c = c + carry_scr[0:1, :]
    carry_scr[0:1, :] = c[tm - 1:tm, :]
    hi = c.astype(BF16).astype(F32)
    mid = (c - hi).astype(BF16).astype(F32)
    lo = (c - hi - mid).astype(BF16).astype(F32)
    slot = lax.broadcasted_iota(jnp.int32, c.shape, 1) % AUG_GROUP
    part = jnp.where(slot % AUG_PARTS == 0, hi, jnp.where(slot % AUG_PARTS == 1, mid, lo))
    used = slot < AUG_SLOTS
    qside = slot < AUG_PARTS
    qaugt_ref[...] = jnp.where(used, jnp.where(qside, part, 1.0), 0.0).T.astype(BF16)
    kaug_ref[...] = jnp.where(used, jnp.where(qside, 1.0, -part), 0.0).astype(BF16)


def _inproj_body(x_ref, g_ref, w_ref, wt_ref, wf_ref, fb_ref, out_ref, outt_ref, qaugt_ref, kaug_ref,
                 h_scr, carry_scr, *, tiles_per_seq):
    @pl.when(pl.program_id(0) % tiles_per_seq == 0)
    def _():
        carry_scr[...] = jnp.zeros(carry_scr.shape, F32)

    h_scr[...] = _rms(x_ref[...], g_ref[...]).astype(BF16)
    _forget_operands(jnp.dot(h_scr[...], wf_ref[...], preferred_element_type=F32), fb_ref[...], carry_scr,
                     qaugt_ref, kaug_ref)
    for j in range(w_ref.shape[1] // TN_PROJ):
        cols = slice(j * TN_PROJ, (j + 1) * TN_PROJ)
        out_ref[:, cols] = jnp.dot(h_scr[...], w_ref[:, cols], preferred_element_type=F32).astype(out_ref.dtype)
    for j in range(wt_ref.shape[0] // TN_PROJ):
        rows = slice(j * TN_PROJ, (j + 1) * TN_PROJ)
        outt_ref[rows, :] = lax.dot_general(wt_ref[rows, :], h_scr[...], (((1,), (1,)), ((), ())),
                                            preferred_element_type=F32).astype(outt_ref.dtype)


def _resident(shape):
    return pl.BlockSpec(shape, lambda *_: (0,) * len(shape), pipeline_mode=pl.Buffered(1))


def _inproj(x2, g, w_row, w_t, w_f, fb, bsz, s_len):
    t, d = x2.shape
    tm = min(TM_PROJ, s_len)
    tps = s_len // tm
    transposed = lambda rows: pl.BlockSpec((None, rows, tm), lambda i: (i // tps, 0, i % tps))
    return pl.pallas_call(
        functools.partial(_inproj_body, tiles_per_seq=tps),
        grid=(t // tm,),
        in_specs=[
            pl.BlockSpec((tm, d), lambda i: (i, 0)),
            _resident(g.shape), _resident(w_row.shape), _resident(w_t.shape), _resident(w_f.shape),
            _resident(fb.shape),
        ],
        out_specs=[
            pl.BlockSpec((tm, N_ROWOUT), lambda i: (i, 0)),
            transposed(N_TOUT),
            transposed(LANES),
            pl.BlockSpec((tm, LANES), lambda i: (i, 0)),
        ],
        out_shape=[
            jax.ShapeDtypeStruct((t, N_ROWOUT), BF16),
            jax.ShapeDtypeStruct((bsz, N_TOUT, s_len), BF16),
            jax.ShapeDtypeStruct((bsz, LANES, s_len), BF16),
            jax.ShapeDtypeStruct((t, LANES), BF16),
        ],
        scratch_shapes=[pltpu.VMEM((tm, d), BF16), pltpu.VMEM((SUBLANES, LANES), F32)],
        compiler_params=pltpu.CompilerParams(dimension_semantics=("arbitrary",), vmem_limit_bytes=VMEM_LIMIT),
        name="inproj",
    )(x2, g, w_row, w_t, w_f, fb)


def _rnn_body(xr_ref, gr_ref, cw_ref, cb_ref, wr_ref, br_ref, wi_ref, bi_ref, ap_ref, y_ref, a_scr, u_scr):
    s, width = xr_ref.shape
    nblk = width // RNN_BLOCK_W
    half = pl.program_id(1)
    zeros = jnp.zeros((SUBLANES, RNN_BLOCK_W), F32)
    for n in range(nblk):
        sl = slice(n * RNN_BLOCK_W, (n + 1) * RNN_BLOCK_W)
        x = xr_ref[:, sl].astype(F32)
        xc = (cb_ref[:, sl] + _shift_rows(x, zeros, 3) * cw_ref[0:1, sl] + _shift_rows(x, zeros, 2) * cw_ref[1:2, sl]
              + _shift_rows(x, zeros, 1) * cw_ref[2:3, sl] + x * cw_ref[3:4, sl])
        xb = xc.astype(BF16)
        blk = half * nblk + n
        zr = jnp.dot(xb, wr_ref[blk], preferred_element_type=F32) + br_ref[:, sl]
        gi = _sigmoid(jnp.dot(xb, wi_ref[blk], preferred_element_type=F32) + bi_ref[:, sl])
        k = (-0.5 * RG_LRU_C) * _softplus(-ap_ref[:, sl])
        a = jnp.exp(k * jnp.tanh(0.5 * zr) + k)
        a_scr[:, sl] = a
        y = 1.0 - a * a
        u_scr[:, sl] = jnp.where(y > 0.0, y * lax.rsqrt(y), 0.0) * (gi * xc)

    row8 = lax.broadcasted_iota(jnp.int32, (SUBLANES, width), 0)

    def group(k, hprev):
        off = pl.multiple_of(k * SUBLANES, SUBLANES)
        a = a_scr[pl.ds(off, SUBLANES), :]
        u = u_scr[pl.ds(off, SUBLANES), :]
        for d in (1, 2, 4):
            a_sh = jnp.where(row8 >= d, pltpu.roll(a, d, 0), 1.0)
            u_sh = jnp.where(row8 >= d, pltpu.roll(u, d, 0), 0.0)
            u = a * u_sh + u
            a = a * a_sh
        h = a * hprev + u
        u_scr[pl.ds(off, SUBLANES), :] = h
        return h[SUBLANES - 1:SUBLANES, :]

    lax.fori_loop(0, s // SUBLANES, group, jnp.zeros((1, width), F32))

    for n in range(nblk):
        sl = slice(n * RNN_BLOCK_W, (n + 1) * RNN_BLOCK_W)
        y_ref[:, sl] = (jax.nn.gelu(gr_ref[:, sl].astype(F32)) * u_scr[:, sl]).astype(y_ref.dtype)


def _rnn(main3, conv_w, conv_b, w_r, b_r, w_i, b_i, a_param):
    b, s, _ = main3.shape
    w = RNN_HALF
    nh = D_RNN // w
    vec = lambda k: pl.BlockSpec((k, w), lambda i, j: (0, j))
    full3 = pl.BlockSpec((RNN_BLOCKS, RNN_BLOCK_W, RNN_BLOCK_W), lambda i, j: (0, 0, 0))
    return pl.pallas_call(
        _rnn_body,
        grid=(b, nh),
        in_specs=[
            pl.BlockSpec((None, s, w), lambda i, j: (i, 0, COL_XRNN // w + j)),
            pl.BlockSpec((None, s, w), lambda i, j: (i, 0, COL_GRNN // w + j)),
            vec(4), vec(1), full3, vec(1), full3, vec(1), vec(1),
        ],
        out_specs=pl.BlockSpec((None, s, w), lambda i, j: (i, 0, j)),
        out_shape=jax.ShapeDtypeStruct((b, s, D_RNN), BF16),
        scratch_shapes=[pltpu.VMEM((s, w), F32), pltpu.VMEM((s, w), F32)],
        compiler_params=pltpu.CompilerParams(
            dimension_semantics=("arbitrary", "arbitrary"), vmem_limit_bytes=VMEM_LIMIT),
        name="rnn",
    )(main3, main3, conv_w, conv_b, w_r, b_r, w_i, b_i, a_param)


def _attend(work, q_tile, k_tile, vt_tile, finish, sc_scr, p_scr, acc_scr):
    maps = range(2)
    tq = sc_scr.shape[-1]
    items = [(i, j, bias, lo, n == 0, n == len(tiles) - 1)
             for i, tiles in enumerate(work) for n, (j, bias, lo) in enumerate(tiles)]
    assert all(lo == 0 for _, _, _, lo, first, _ in items if first)

    def scores(item, slot):
        i, j, bias, lo, _, _ = item
        tops = []
        for w in maps:
            sc = jnp.dot(k_tile(j, w), q_tile(i, w)[:, lo:], preferred_element_type=F32)
            if bias is not None:
                sc = sc + bias[:, lo:]
            sc_scr[slot, w, :, lo:] = sc
            tops.append(jnp.max(sc, axis=0, keepdims=True))
        return tuple(tops)

    def values(item, slot, alphas):
        i, j, _, lo, first, last = item
        for w in maps:
            pv = jnp.dot(vt_tile(j, w), p_scr[slot, w, :, lo:], preferred_element_type=F32)
            acc_scr[w, :, lo:] = pv if first else alphas[w] * acc_scr[w, :, lo:] + pv
        if last:
            finish(i, acc_scr[0], acc_scr[1])

    ms = alphas = None
    tops = scores(items[0], 0)
    for g, item in enumerate(items):
        slot = g % 2
        if g > 0:
            values(items[g - 1], 1 - slot, alphas)
        next_tops = scores(items[g + 1], 1 - slot) if g + 1 < len(items) else None
        lo, first = item[3], item[4]
        if first:
            ms = (jnp.full((1, tq), NEG, F32),) * 2
        new_ms, alphas = [], []
        for w in maps:
            m_old = ms[w][:, lo:]
            m_new = jnp.maximum(m_old, tops[w])
            alphas.append(jnp.exp2(m_old - m_new))
            p_scr[slot, w, :, lo:] = jnp.exp2(sc_scr[slot, w, :, lo:] - m_new).astype(BF16)
            new_ms.append(m_new if lo == 0 else jnp.concatenate([ms[w][:, :lo], m_new], axis=1))
        ms, tops = new_ms, next_tops
    values(items[-1], (len(items) - 1) % 2, alphas)


def _diff_body(rel_ref, qt_ref, k_ref, vt_ref, lq1_ref, lk1_ref, lq2_ref, lk2_ref, sg_ref, o_ref,
               bias_scr, qa_scr, qb_scr, vx_scr, sc_scr, p_scr, acc_scr, *, lam_init):
    s = k_ref.shape[0]
    head = pl.program_id(0)

    @pl.when(pl.program_id(1) == 0)
    def _():
        ki = lax.broadcasted_iota(jnp.int32, (TK, TQ), 0)
        qi = lax.broadcasted_iota(jnp.int32, (TK, TQ), 1)
        far = rel_ref[N_BUCKETS - 1, head]
        for idx in range(3):
            dist = qi - ki + (1 - idx) * TK
            n = jnp.maximum(dist, 0)
            nf = jnp.maximum(n, 1).astype(F32)
            large = MAX_EXACT + (jnp.log(nf / MAX_EXACT) / math.log(MAX_DISTANCE / MAX_EXACT)
                                 * (N_BUCKETS - MAX_EXACT)).astype(jnp.int32)
            large = jnp.minimum(large, N_BUCKETS - 1)
            bucket = jnp.where(n < MAX_EXACT, n, large)
            bias = jnp.zeros((TK, TQ), F32)
            for bk in range(N_BUCKETS):
                bias = jnp.where(bucket == bk, rel_ref[bk, head], bias)
            bias_scr[idx] = jnp.where(dist >= 0, (bias - far) * LOG2E, NEG)

    rowi = lax.broadcasted_iota(jnp.int32, qt_ref.shape, 0)
    qt = qt_ref[...].astype(F32)
    qa_scr[...] = jnp.where(rowi < DIFF_HEAD_DIM, qt, 0.0).astype(BF16)
    qb_scr[...] = jnp.where(rowi < DIFF_HEAD_DIM, 0.0, qt).astype(BF16)
    vx_scr[0:LANES, :] = vt_ref[...]
    vx_scr[LANES:, :] = jnp.ones((BF16_ROWS, s), BF16)

    lam = (jnp.exp(jnp.sum(lq1_ref[...] * lk1_ref[...], axis=-1, keepdims=True))
           - jnp.exp(jnp.sum(lq2_ref[...] * lk2_ref[...], axis=-1, keepdims=True)) + lam_init)

    k_tile = lambda j, w: k_ref[pl.ds(j * TK, TK), :]
    vt_tile = lambda j, w: vx_scr[:, pl.ds(j * TK, TK)]
    q_tile = lambda i, w: (qa_scr, qb_scr)[w][:, i * TQ:(i + 1) * TQ]
    work = []
    for i in range(s // TQ):
        j0 = i * (TQ // TK)
        near = [(j0 - 1 + idx, bias_scr.at[idx], max(idx - 1, 0) * TK) for idx in range(3) if j0 - 1 + idx >= 0]
        work.append([(j, None, 0) for j in range(j0 - 1)] + near)

    def finish(i, acca, accb):
        o = (acca[:LANES] / acca[LANES:LANES + 1] - lam * (accb[:LANES] / accb[LANES:LANES + 1]))
        o = o * lax.rsqrt(jnp.mean(o * o, axis=0, keepdims=True) + EPS)
        o_ref[i * TQ:(i + 1) * TQ, :] = (o.T * sg_ref[...] * (1.0 - lam_init)).astype(o_ref.dtype)

    _attend(work, q_tile, k_tile, vt_tile, finish, sc_scr, p_scr, acc_scr)


def _diff_attn(main3, maint, rel_bias, lq1, lk1, lq2, lk2, subln_g, lam_init):
    b, s, _ = main3.shape
    small = lambda w: pl.BlockSpec((1, w), lambda h, i: (0, 0))
    trow = lambda base: pl.BlockSpec((None, LANES, s), lambda h, i: (i, base // LANES + h, 0))
    return pl.pallas_call(
        functools.partial(_diff_body, lam_init=lam_init),
        grid=(DIFF_HEADS, b),
        in_specs=[
            pl.BlockSpec(memory_space=pltpu.SMEM),
            trow(ROW_DQ),
            pl.BlockSpec((None, s, LANES), lambda h, i: (i, 0, COL_DK // LANES + h)),
            trow(ROW_DV),
            small(DIFF_HEAD_DIM), small(DIFF_HEAD_DIM), small(DIFF_HEAD_DIM), small(DIFF_HEAD_DIM),
            small(2 * DIFF_HEAD_DIM),
        ],
        out_specs=pl.BlockSpec((None, s, LANES), lambda h, i: (i, 0, h)),
        out_shape=jax.ShapeDtypeStruct((b, s, DIFF_WIDTH), BF16),
        scratch_shapes=[
            pltpu.VMEM((3, TK, TQ), F32),
            pltpu.VMEM((LANES, s), BF16),
            pltpu.VMEM((LANES, s), BF16),
            pltpu.VMEM((LANES + BF16_ROWS, s), BF16),
            pltpu.VMEM((2, 2, TK, TQ), F32),
            pltpu.VMEM((2, 2, TK, TQ), BF16),
            pltpu.VMEM((2, LANES + BF16_ROWS, TQ), F32),
        ],
        compiler_params=pltpu.CompilerParams(
            dimension_semantics=("arbitrary", "arbitrary"), vmem_limit_bytes=VMEM_LIMIT),
        name="diff_attn",
    )(rel_bias, maint, main3, maint, lq1, lk1, lq2, lk2, subln_g)


def _fox_body(qt_ref, k_ref, vt_ref, qaugt_ref, kaug_ref, o_ref,
              mask_scr, qa_scr, qb_scr, ka_scr, kb_scr, va_scr, vb_scr, sc_scr, p_scr, acc_scr):
    s = k_ref.shape[0]
    pair = pl.program_id(0)

    @pl.when(pl.program_id(1) == 0)
    def _():
        ki = lax.broadcasted_iota(jnp.int32, (TK, TQ), 0)
        qi = lax.broadcasted_iota(jnp.int32, (TK, TQ), 1)
        for idx in range(TQ // TK):
            mask_scr[idx] = jnp.where(qi >= ki + idx * TK, 0.0, NEG)

    half = FOX_HEAD_DIM
    lo_a, lo_b = half + AUG_GROUP * pair, AUG_GROUP * pair

    def split(x, aug, idx):
        is_a = idx < half
        aug_a = jnp.where((idx >= lo_a) & (idx < lo_a + AUG_SLOTS), aug, 0.0)
        aug_b = jnp.where((idx >= lo_b) & (idx < lo_b + AUG_SLOTS), aug, 0.0)
        return jnp.where(is_a, x, aug_a).astype(BF16), jnp.where(is_a, aug_b, x).astype(BF16)

    rowi = lax.broadcasted_iota(jnp.int32, qt_ref.shape, 0)
    lanei = lax.broadcasted_iota(jnp.int32, k_ref.shape, 1)
    qa_scr[...], qb_scr[...] = split(qt_ref[...].astype(F32), qaugt_ref[...].astype(F32), rowi)
    ka_scr[...], kb_scr[...] = split(k_ref[...].astype(F32), kaug_ref[...].astype(F32), lanei)
    vt = vt_ref[...].astype(F32)
    va_scr[...] = jnp.where(rowi < half, vt, 1.0).astype(BF16)
    vb_scr[...] = jnp.where(rowi < half, 1.0, vt).astype(BF16)

    k_tile = lambda j, w: (ka_scr, kb_scr)[w][pl.ds(j * TK, TK), :]
    vt_tile = lambda j, w: (va_scr, vb_scr)[w][:, pl.ds(j * TK, TK)]
    rows = lax.broadcasted_iota(jnp.int32, (LANES, TQ), 0)
    q_tile = lambda i, w: (qa_scr, qb_scr)[w][:, i * TQ:(i + 1) * TQ]
    work = []
    for i in range(s // TQ):
        j0 = i * (TQ // TK)
        work.append([(j, None, 0) for j in range(j0)]
                    + [(j0 + idx, mask_scr.at[idx], idx * TK) for idx in range(TQ // TK)])

    def finish(i, acca, accb):
        o = jnp.where(rows < half, acca / acca[half:half + 1], accb / accb[0:1])
        o_ref[i * TQ:(i + 1) * TQ, :] = o.T.astype(o_ref.dtype)

    _attend(work, q_tile, k_tile, vt_tile, finish, sc_scr, p_scr, acc_scr)


def _fox_attn(main3, maint, qaugt, kaug):
    b, s, _ = main3.shape
    trow = lambda base: pl.BlockSpec((None, LANES, s), lambda p, i: (i, base // LANES + p, 0))
    tsc = pltpu.VMEM((LANES, s), BF16)
    return pl.pallas_call(
        _fox_body,
        grid=(FOX_HEADS // 2, b),
        in_specs=[
            trow(ROW_FQ),
            pl.BlockSpec((None, s, LANES), lambda p, i: (i, 0, COL_FK // LANES + p)),
            trow(ROW_FV),
            pl.BlockSpec((None, LANES, s), lambda p, i: (i, 0, 0)),
            pl.BlockSpec((None, s, LANES), lambda p, i: (i, 0, 0)),
        ],
        out_specs=pl.BlockSpec((None, s, LANES), lambda p, i: (i, 0, p)),
        out_shape=jax.ShapeDtypeStruct((b, s, FOX_WIDTH), BF16),
        scratch_shapes=[
            pltpu.VMEM((TQ // TK, TK, TQ), F32),
            tsc, tsc, pltpu.VMEM((s, LANES), BF16), pltpu.VMEM((s, LANES), BF16), tsc, tsc,
            pltpu.VMEM((2, 2, TK, TQ), F32),
            pltpu.VMEM((2, 2, TK, TQ), BF16),
            pltpu.VMEM((2, LANES, TQ), F32),
        ],
        compiler_params=pltpu.CompilerParams(
            dimension_semantics=("arbitrary", "arbitrary"), vmem_limit_bytes=VMEM_LIMIT),
        name="fox_attn",
    )(maint, main3, maint, qaugt, kaug)


def _merge_body(x_ref, g0_ref, g1_ref, g2_ref, gb_ref, yr_ref, yd_ref, yf_ref, wr_ref, wd_ref, wf_ref, wo_ref,
                o_ref):
    def branch(g_ref, k, y_ref, w_ref):
        gate = jax.nn.sigmoid(g_ref[...].astype(F32) + gb_ref[k:k + 1, :])
        return gate * jnp.dot(y_ref[...], w_ref[...], preferred_element_type=F32)

    m = branch(g0_ref, 0, yr_ref, wr_ref) + branch(g1_ref, 1, yd_ref, wd_ref) + branch(g2_ref, 2, yf_ref, wf_ref)
    o_ref[...] = x_ref[...] + jnp.dot(m.astype(BF16), wo_ref[...], preferred_element_type=F32)


def _merge(x2, main2, gate_b, y_rnn, y_diff, y_fox, w_r, w_d, w_f, w_o):
    t, d = x2.shape
    tm = min(TM_MERGE, t)
    rows = lambda w, c=0: pl.BlockSpec((tm, w), lambda i: (i, c))
    whole = lambda a: pl.BlockSpec(a.shape, lambda i: (0, 0))
    g0 = COL_GATES // d
    return pl.pallas_call(
        _merge_body,
        grid=(t // tm,),
        in_specs=[
            rows(d), rows(d, g0), rows(d, g0 + 1), rows(d, g0 + 2), whole(gate_b),
            rows(D_RNN), rows(DIFF_WIDTH), rows(FOX_WIDTH),
            whole(w_r), whole(w_d), whole(w_f), whole(w_o),
        ],
        out_specs=rows(d),
        out_shape=jax.ShapeDtypeStruct((t, d), F32),
        compiler_params=pltpu.CompilerParams(dimension_semantics=("arbitrary",), vmem_limit_bytes=VMEM_LIMIT),
        name="merge",
    )(x2, main2, main2, main2, gate_b, y_rnn, y_diff, y_fox, w_r, w_d, w_f, w_o)


def _ffn_body(x_ref, g_ref, wu_ref, cw_ref, cb_ref, wd_ref, gout_ref, o_ref,
              h_scr, acc_scr, act_scr, carry_scr, *, n_tiles, tiles_per_seq, out_norm):
    i = pl.program_id(0)
    tm = x_ref.shape[0]
    part = lambda c, off=0: pl.ds(off + FF_SPLIT[c], FF_SPLIT[c + 1] - FF_SPLIT[c])

    @pl.when(i == 0)
    def _():
        act_scr[...] = jnp.zeros(act_scr.shape, BF16)
        acc_scr[...] = jnp.zeros(acc_scr.shape, F32)

    @pl.when((jnp.minimum(i, n_tiles - 1) % tiles_per_seq) == 0)
    def _():
        carry_scr[...] = jnp.zeros(carry_scr.shape, F32)

    h_scr[...] = _rms(x_ref[...], g_ref[...]).astype(BF16)

    def up(c, off):
        return jnp.dot(h_scr[...], wu_ref[:, part(c, off)], preferred_element_type=F32)

    def activation(us, c):
        out = []
        for u, off in zip(us, (0, D_FF)):
            cols = part(c, off)
            before = carry_scr[:, cols]
            carry_scr[:, cols] = u[tm - SUBLANES:, :]
            out.append(cb_ref[:, cols] + _shift_rows(u, before, 2) * cw_ref[0:1, cols]
                       + _shift_rows(u, before, 1) * cw_ref[1:2, cols] + u * cw_ref[2:3, cols])
        return (jax.nn.gelu(out[0]) * out[1]).astype(BF16)

    down = lambda act, c: jnp.dot(act, wd_ref[part(c), :], preferred_element_type=F32)

    ug0 = up(0, 0)
    done = acc_scr[...] + down(act_scr[...], 1)
    o_ref[...] = _rms(done, gout_ref[...]) if out_norm else done
    uv0 = up(0, D_FF)
    act0 = activation((ug0, uv0), 0)
    ug1 = up(1, 0)
    acc_scr[...] = x_ref[...] + down(act0, 0)
    uv1 = up(1, D_FF)
    act_scr[...] = activation((ug1, uv1), 1)


def _ffn(x2, g, w_up, conv_w, conv_b, w_down, g_out, seq_len, out_norm):
    t, d = x2.shape
    tm = min(TM_FFN, seq_len)
    n_tiles = t // tm
    return pl.pallas_call(
        functools.partial(_ffn_body, n_tiles=n_tiles, tiles_per_seq=seq_len // tm, out_norm=out_norm),
        grid=(n_tiles + 1,),
        in_specs=[
            pl.BlockSpec((tm, d), lambda i: (jnp.minimum(i, n_tiles - 1), 0)),
            _resident(g.shape), _resident(w_up.shape), _resident(conv_w.shape), _resident(conv_b.shape),
            _resident(w_down.shape), _resident(g_out.shape),
        ],
        out_specs=pl.BlockSpec((tm, d), lambda i: (jnp.maximum(i - 1, 0), 0)),
        out_shape=jax.ShapeDtypeStruct((t, d), F32),
        scratch_shapes=[
            pltpu.VMEM((tm, d), BF16),
            pltpu.VMEM((tm, d), F32),
            pltpu.VMEM((tm, FF_SPLIT[2] - FF_SPLIT[1]), BF16),
            pltpu.VMEM((SUBLANES, 2 * D_FF), F32),
        ],
        compiler_params=pltpu.CompilerParams(dimension_semantics=("arbitrary",), vmem_limit_bytes=VMEM_LIMIT),
        name="ffn",
    )(x2, g, w_up, conv_w, conv_b, w_down, g_out)


def _aug_head_of_lane():
    heads = []
    for lane in range(LANES):
        group, slot = (lane % (LANES // 2)) // AUG_GROUP, lane % AUG_GROUP
        if group < FOX_HEADS // 2 and slot < AUG_SLOTS:
            heads.append(2 * group + (1 if lane < LANES // 2 else 0))
        else:
            heads.append(-1)
    return heads


def _prep_w_in(w_in, fox_b_f):
    colscale = jnp.ones((w_in.shape[-1],), F32)
    colscale = colscale.at[W_DQ:W_DQ + DIFF_QK].set(DIFF_HEAD_DIM ** -0.5 * LOG2E)
    colscale = colscale.at[W_FQ:W_FQ + FOX_WIDTH].set(FOX_HEAD_DIM ** -0.5 * LOG2E)
    wb = (w_in * colscale).astype(BF16)
    heads = _aug_head_of_lane()
    idx = jnp.array([max(h, 0) for h in heads], jnp.int32)
    live = jnp.array([1.0 if h >= 0 else 0.0 for h in heads], F32)
    out = []
    for l in range(w_in.shape[0]):
        seg = lambda a, n, l=l: wb[l, :, a:a + n]
        w_row = jnp.concatenate([seg(W_XRNN, D_RNN), seg(W_GRNN, D_RNN), seg(W_DK, DIFF_QK), seg(W_FK, FOX_WIDTH),
                                 seg(W_GATES, N_BRANCH * D_MODEL)], axis=1)
        w_t = jnp.concatenate([seg(W_DQ, DIFF_QK), seg(W_FQ, FOX_WIDTH), seg(W_DV, DIFF_WIDTH),
                               seg(W_FV, FOX_WIDTH)], axis=1).T
        w_f = (w_in[l, :, W_FLOG:W_FLOG + FOX_HEADS][:, idx] * live).astype(BF16)
        fb = (fox_b_f[l][idx] * live).reshape(1, LANES)
        out.append((w_row, w_t, w_f, fb))
    return out


def kernel(x, norm1_g, w_in, rnn_conv_w, rnn_conv_b, rg_w_r, rg_b_r, rg_w_i, rg_b_i, rg_a,
           diff_lq1, diff_lk1, diff_lq2, diff_lk2, diff_subln_g, rel_bias, fox_b_f, gate_b,
           w_br_rnn, w_br_diff, w_br_fox, w_out, norm2_g, ffn_up, ffn_conv_w, ffn_conv_b,
           ffn_down, final_g):
    bsz, s_len, d = x.shape
    t = bsz * s_len
    x2 = x.reshape(t, d)
    row = lambda v: v.reshape(1, -1)
    w_in_parts = _prep_w_in(w_in, fox_b_f)
    rg_w_r, rg_w_i, w_br_rnn, w_br_diff, w_br_fox, w_out, ffn_up, ffn_down = (
        w.astype(BF16) for w in (rg_w_r, rg_w_i, w_br_rnn, w_br_diff, w_br_fox, w_out, ffn_up, ffn_down))
    for l in range(DEPTH):
        w_row, w_t, w_f, fb = w_in_parts[l]
        main2, maint, qaugt, kaug = _inproj(x2, row(norm1_g[l]), w_row, w_t, w_f, fb, bsz, s_len)
        main3 = main2.reshape(bsz, s_len, N_ROWOUT)

        y_rnn = _rnn(main3, rnn_conv_w[l], row(rnn_conv_b[l]), rg_w_r[l], row(rg_b_r[l]),
                     rg_w_i[l], row(rg_b_i[l]), row(rg_a[l]))

        lam_init = 0.8 - 0.6 * math.exp(-0.3 * l)
        y_diff = _diff_attn(main3, maint, rel_bias, row(diff_lq1[l]), row(diff_lk1[l]), row(diff_lq2[l]),
                            row(diff_lk2[l]), row(diff_subln_g[l]), lam_init)

        y_fox = _fox_attn(main3, maint, qaugt, kaug.reshape(bsz, s_len, LANES))

        x2 = _merge(x2, main2, gate_b[l], y_rnn.reshape(t, D_RNN), y_diff.reshape(t, DIFF_WIDTH),
                    y_fox.reshape(t, FOX_WIDTH), w_br_rnn[l], w_br_diff[l], w_br_fox[l], w_out[l])

        x2 = _ffn(x2, row(norm2_g[l]), ffn_up[l], ffn_conv_w[l], row(ffn_conv_b[l]),
                  ffn_down[l], row(final_g), s_len, out_norm=(l == DEPTH - 1))
    return x2.reshape(bsz, s_len, d)
```

```python
import functools
import math

import jax
import jax.numpy as jnp
from jax import lax
from jax.experimental import pallas as pl
from jax.experimental.pallas import tpu as pltpu

F32 = jnp.float32
BF16 = jnp.bfloat16

D_MODEL = 1024
DEPTH = 2
D_RNN = D_MODEL
RNN_BLOCKS = 8
RNN_BLOCK_W = D_RNN // RNN_BLOCKS
RG_LRU_C = 8.0
DIFF_HEADS = 4
DIFF_HEAD_DIM = 64
DIFF_QK = DIFF_HEADS * 2 * DIFF_HEAD_DIM
DIFF_WIDTH = DIFF_HEADS * 2 * DIFF_HEAD_DIM
FOX_HEADS = 8
FOX_HEAD_DIM = 64
FOX_WIDTH = FOX_HEADS * FOX_HEAD_DIM
N_BUCKETS = 32
MAX_EXACT = N_BUCKETS // 2
MAX_DISTANCE = 128
D_FF = ((8 * D_MODEL // 3 + 127) // 128) * 128
N_BRANCH = 3
EPS = 1e-6
LOG2E = 1.4426950408889634
NEG = -1e30

LANES = 128
SUBLANES = 8
BF16_ROWS = 16

W_XRNN = 0
W_GRNN = W_XRNN + D_RNN
W_DQ = W_GRNN + D_RNN
W_DK = W_DQ + DIFF_QK
W_DV = W_DK + DIFF_QK
W_FQ = W_DV + DIFF_WIDTH
W_FK = W_FQ + FOX_WIDTH
W_FV = W_FK + FOX_WIDTH
W_FLOG = W_FV + FOX_WIDTH
W_GATES = W_FLOG + FOX_HEADS

COL_XRNN = 0
COL_GRNN = COL_XRNN + D_RNN
COL_DK = COL_GRNN + D_RNN
COL_FK = COL_DK + DIFF_QK
COL_GATES = COL_FK + FOX_WIDTH
N_ROWOUT = COL_GATES + N_BRANCH * D_MODEL
ROW_DQ = 0
ROW_FQ = ROW_DQ + DIFF_QK
ROW_DV = ROW_FQ + FOX_WIDTH
ROW_FV = ROW_DV + DIFF_WIDTH
N_TOUT = ROW_FV + FOX_WIDTH

TQ = 512
TK = 256
TM_PROJ = 512
TN_PROJ = 1024
TM_MERGE = 512
TM_FFN = 512
MXU_TILE = 256
FF_SPLIT = (0, (D_FF // MXU_TILE + 1) // 2 * MXU_TILE, D_FF)
RNN_HALF = D_RNN // 2
VMEM_LIMIT = 56 * 1024 * 1024

AUG_GROUP = 8
AUG_SLOTS = 6
AUG_PARTS = 3


def _softplus(z):
    return jnp.maximum(z, 0.0) + jnp.log1p(jnp.exp(-jnp.abs(z)))


def _rms(x, g):
    return x * lax.rsqrt(jnp.mean(x * x, axis=-1, keepdims=True) + EPS) * g


def _sigmoid(z):
    return 0.5 * jnp.tanh(0.5 * z) + 0.5


def _shift_rows(u, before, k):
    rows, width = u.shape
    groups = rows // SUBLANES
    rot = pltpu.roll(u.reshape(groups, SUBLANES, width), k, 1)
    prev = jnp.concatenate([pltpu.roll(before, k, 0)[None], rot[:-1]], axis=0)
    sub = lax.broadcasted_iota(jnp.int32, rot.shape, 1)
    return jnp.where(sub < k, prev, rot).reshape(rows, width)


def _forget_operands(f_logit, fb, carry_scr, qaugt_ref, kaug_ref):
    tm = f_logit.shape[0]
    c = -_softplus(-(f_logit + fb)) * LOG2E
    row = lax.broadcasted_iota(jnp.int32, c.shape, 0)
    d = 1
    while d < tm:
        c = c + jnp.where(row >= d, pltpu.roll(c, d, 0), 0.0)
        d *= 2
    c = c + carry_scr[0:1, :]
    carry_scr[0:1, :] = c[tm - 1:tm, :]
    hi = c.astype(BF16).astype(F32)
    mid = (c - hi).astype(BF16).astype(F32)
    lo = (c - hi - mid).astype(BF16).astype(F32)
    slot = lax.broadcasted_iota(jnp.int32, c.shape, 1) % AUG_GROUP
    part = jnp.where(slot % AUG_PARTS == 0, hi, jnp.where(slot % AUG_PARTS == 1, mid, lo))
    used = slot < AUG_SLOTS
    qside = slot < AUG_PARTS
    qaugt_ref[...] = jnp.where(used, jnp.where(qside, part, 1.0), 0.0).T.astype(BF16)
    kaug_ref[...] = jnp.where(used, jnp.where(qside, 1.0, -part), 0.0).astype(BF16)


def _inproj_body(x_ref, g_ref, w_ref, wt_ref, wf_ref, fb_ref, out_ref, outt_ref, qaugt_ref, kaug_ref,
                 h_scr, carry_scr, *, tiles_per_seq):
    @pl.when(pl.program_id(0) % tiles_per_seq == 0)
    def _():
        carry_scr[...] = jnp.zeros(carry_scr.shape, F32)

    h_scr[...] = _rms(x_ref[...], g_ref[...]).astype(BF16)
    _forget_operands(jnp.dot(h_scr[...], wf_ref[...], preferred_element_type=F32), fb_ref[...], carry_scr,
                     qaugt_ref, kaug_ref)
    for j in range(w_ref.shape[1] // TN_PROJ):
        cols = slice(j * TN_PROJ, (j + 1) * TN_PROJ)
        out_ref[:, cols] = jnp.dot(h_scr[...], w_ref[:, cols], preferred_element_type=F32).astype(out_ref.dtype)
    for j in range(wt_ref.shape[0] // TN_PROJ):
        rows = slice(j * TN_PROJ, (j + 1) * TN_PROJ)
        outt_ref[rows, :] = lax.dot_general(wt_ref[rows, :], h_scr[...], (((1,), (1,)), ((), ())),
                                            preferred_element_type=F32).astype(outt_ref.dtype)


def _resident(shape):
    return pl.BlockSpec(shape, lambda *_: (0,) * len(shape), pipeline_mode=pl.Buffered(1))


def _inproj(x2, g, w_row, w_t, w_f, fb, bsz, s_len):
    t, d = x2.shape
    tm = min(TM_PROJ, s_len)
    tps = s_len // tm
    transposed = lambda rows: pl.BlockSpec((None, rows, tm), lambda i: (i // tps, 0, i % tps))
    return pl.pallas_call(
        functools.partial(_inproj_body, tiles_per_seq=tps),
        grid=(t // tm,),
        in_specs=[
            pl.BlockSpec((tm, d), lambda i: (i, 0)),
            _resident(g.shape), _resident(w_row.shape), _resident(w_t.shape), _resident(w_f.shape),
            _resident(fb.shape),
        ],
        out_specs=[
            pl.BlockSpec((tm, N_ROWOUT), lambda i: (i, 0)),
            transposed(N_TOUT),
            transposed(LANES),
            pl.BlockSpec((tm, LANES), lambda i: (i, 0)),
        ],
        out_shape=[
            jax.ShapeDtypeStruct((t, N_ROWOUT), BF16),
            jax.ShapeDtypeStruct((bsz, N_TOUT, s_len), BF16),
            jax.ShapeDtypeStruct((bsz, LANES, s_len), BF16),
            jax.ShapeDtypeStruct((t, LANES), BF16),
        ],
        scratch_shapes=[pltpu.VMEM((tm, d), BF16), pltpu.VMEM((SUBLANES, LANES), F32)],
        compiler_params=pltpu.CompilerParams(dimension_semantics=("arbitrary",), vmem_limit_bytes=VMEM_LIMIT),
        name="inproj",
    )(x2, g, w_row, w_t, w_f, fb)


def _rnn_body(xr_ref, cw_ref, cb_ref, wr_ref, br_ref, wi_ref, bi_ref, ap_ref, y_ref, a_scr, u_scr):
    s, width = xr_ref.shape
    nblk = width // RNN_BLOCK_W
    half = pl.program_id(1)
    zeros = jnp.zeros((SUBLANES, RNN_BLOCK_W), F32)
    for n in range(nblk):
        sl = slice(n * RNN_BLOCK_W, (n + 1) * RNN_BLOCK_W)
        x = xr_ref[:, sl].astype(F32)
        xc = (cb_ref[:, sl] + _shift_rows(x, zeros, 3) * cw_ref[0:1, sl] + _shift_rows(x, zeros, 2) * cw_ref[1:2, sl]
              + _shift_rows(x, zeros, 1) * cw_ref[2:3, sl] + x * cw_ref[3:4, sl])
        xb = xc.astype(BF16)
        blk = half * nblk + n
        zr = jnp.dot(xb, wr_ref[blk], preferred_element_type=F32) + br_ref[:, sl]
        gi = _sigmoid(jnp.dot(xb, wi_ref[blk], preferred_element_type=F32) + bi_ref[:, sl])
        k = (-0.5 * RG_LRU_C) * _softplus(-ap_ref[:, sl])
        a = jnp.exp(k * jnp.tanh(0.5 * zr) + k)
        a_scr[:, sl] = a
        y = 1.0 - a * a
        u_scr[:, sl] = jnp.where(y > 0.0, y * lax.rsqrt(y), 0.0) * (gi * xc)

    row8 = lax.broadcasted_iota(jnp.int32, (SUBLANES, width), 0)

    def group(k, hprev):
        off = pl.multiple_of(k * SUBLANES, SUBLANES)
        a = a_scr[pl.ds(off, SUBLANES), :]
        u = u_scr[pl.ds(off, SUBLANES), :]
        for d in (1, 2, 4):
            a_sh = jnp.where(row8 >= d, pltpu.roll(a, d, 0), 1.0)
            u_sh = jnp.where(row8 >= d, pltpu.roll(u, d, 0), 0.0)
            u = a * u_sh + u
            a = a * a_sh
        h = a * hprev + u
        u_scr[pl.ds(off, SUBLANES), :] = h
        return h[SUBLANES - 1:SUBLANES, :]

    lax.fori_loop(0, s // SUBLANES, group, jnp.zeros((1, width), F32), unroll=4)

    y_ref[...] = u_scr[...].astype(y_ref.dtype)


def _rnn(main3, conv_w, conv_b, w_r, b_r, w_i, b_i, a_param):
    b, s, _ = main3.shape
    w = RNN_HALF
    nh = D_RNN // w
    vec = lambda k: pl.BlockSpec((k, w), lambda i, j: (0, j))
    full3 = pl.BlockSpec((RNN_BLOCKS, RNN_BLOCK_W, RNN_BLOCK_W), lambda i, j: (0, 0, 0))
    return pl.pallas_call(
        _rnn_body,
        grid=(b, nh),
        in_specs=[
            pl.BlockSpec((None, s, w), lambda i, j: (i, 0, COL_XRNN // w + j)),
            vec(4), vec(1), full3, vec(1), full3, vec(1), vec(1),
        ],
        out_specs=pl.BlockSpec((None, s, w), lambda i, j: (i, 0, j)),
        out_shape=jax.ShapeDtypeStruct((b, s, D_RNN), BF16),
        scratch_shapes=[pltpu.VMEM((s, w), F32), pltpu.VMEM((s, w), F32)],
        compiler_params=pltpu.CompilerParams(
            dimension_semantics=("arbitrary", "arbitrary"), vmem_limit_bytes=VMEM_LIMIT),
        name="rnn",
    )(main3, conv_w, conv_b, w_r, b_r, w_i, b_i, a_param)


def _attend(work, q_tile, k_tile, vt_tile, finish, sc_scr, p_scr, acc_scr):
    maps = range(2)
    tq = sc_scr.shape[-1]
    items = [(i, j, bias, lo, n == 0, n == len(tiles) - 1)
             for i, tiles in enumerate(work) for n, (j, bias, lo) in enumerate(tiles)]
    assert all(lo == 0 for _, _, _, lo, first, _ in items if first)

    def scores(item, slot):
        i, j, bias, lo, _, _ = item
        tops = []
        for w in maps:
            sc = jnp.dot(k_tile(j, w), q_tile(i, w)[:, lo:], preferred_element_type=F32)
            if bias is not None:
                sc = sc + bias[:, lo:]
            sc_scr[slot, w, :, lo:] = sc
            tops.append(jnp.max(sc, axis=0, keepdims=True))
        return tuple(tops)

    def values(item, slot, alphas):
        i, j, _, lo, first, last = item
        for w in maps:
            pv = jnp.dot(vt_tile(j, w), p_scr[slot, w, :, lo:], preferred_element_type=F32)
            acc_scr[w, :, lo:] = pv if first else alphas[w] * acc_scr[w, :, lo:] + pv
        if last:
            finish(i, acc_scr[0], acc_scr[1])

    ms = alphas = None
    tops = scores(items[0], 0)
    for g, item in enumerate(items):
        slot = g % 2
        if g > 0:
            values(items[g - 1], 1 - slot, alphas)
        next_tops = scores(items[g + 1], 1 - slot) if g + 1 < len(items) else None
        lo, first = item[3], item[4]
        if first:
            ms = (jnp.full((1, tq), NEG, F32),) * 2
        new_ms, alphas = [], []
        for w in maps:
            m_old = ms[w][:, lo:]
            m_new = jnp.maximum(m_old, tops[w])
            alphas.append(jnp.exp2(m_old - m_new))
            p_scr[slot, w, :, lo:] = jnp.exp2(sc_scr[slot, w, :, lo:] - m_new).astype(BF16)
            new_ms.append(m_new if lo == 0 else jnp.concatenate([ms[w][:, :lo], m_new], axis=1))
        ms, tops = new_ms, next_tops
    values(items[-1], (len(items) - 1) % 2, alphas)


def _diff_body(rel_ref, qt_ref, k_ref, vt_ref, lq1_ref, lk1_ref, lq2_ref, lk2_ref, sg_ref, o_ref,
               bias_scr, qa_scr, qb_scr, vx_scr, sc_scr, p_scr, acc_scr, *, lam_init):
    s = k_ref.shape[0]
    head = pl.program_id(0)

    @pl.when(pl.program_id(1) == 0)
    def _():
        ki = lax.broadcasted_iota(jnp.int32, (TK, TQ), 0)
        qi = lax.broadcasted_iota(jnp.int32, (TK, TQ), 1)
        far = rel_ref[N_BUCKETS - 1, head]
        for idx in range(3):
            dist = qi - ki + (1 - idx) * TK
            n = jnp.maximum(dist, 0)
            nf = jnp.maximum(n, 1).astype(F32)
            large = MAX_EXACT + (jnp.log(nf / MAX_EXACT) / math.log(MAX_DISTANCE / MAX_EXACT)
                                 * (N_BUCKETS - MAX_EXACT)).astype(jnp.int32)
            large = jnp.minimum(large, N_BUCKETS - 1)
            bucket = jnp.where(n < MAX_EXACT, n, large)
            bias = jnp.zeros((TK, TQ), F32)
            for bk in range(N_BUCKETS):
                bias = jnp.where(bucket == bk, rel_ref[bk, head], bias)
            bias_scr[idx] = jnp.where(dist >= 0, (bias - far) * LOG2E, NEG)

    rowi = lax.broadcasted_iota(jnp.int32, qt_ref.shape, 0)
    qt = qt_ref[...].astype(F32)
    qa_scr[...] = jnp.where(rowi < DIFF_HEAD_DIM, qt, 0.0).astype(BF16)
    qb_scr[...] = jnp.where(rowi < DIFF_HEAD_DIM, 0.0, qt).astype(BF16)
    vx_scr[0:LANES, :] = vt_ref[...]
    vx_scr[LANES:, :] = jnp.ones((BF16_ROWS, s), BF16)

    lam = (jnp.exp(jnp.sum(lq1_ref[...] * lk1_ref[...], axis=-1, keepdims=True))
           - jnp.exp(jnp.sum(lq2_ref[...] * lk2_ref[...], axis=-1, keepdims=True)) + lam_init)

    k_tile = lambda j, w: k_ref[pl.ds(j * TK, TK), :]
    vt_tile = lambda j, w: vx_scr[:, pl.ds(j * TK, TK)]
    q_tile = lambda i, w: (qa_scr, qb_scr)[w][:, i * TQ:(i + 1) * TQ]
    work = []
    for i in range(s // TQ):
        j0 = i * (TQ // TK)
        near = [(j0 - 1 + idx, bias_scr.at[idx], max(idx - 1, 0) * TK) for idx in range(3) if j0 - 1 + idx >= 0]
        work.append([(j, None, 0) for j in range(j0 - 1)] + near)

    def finish(i, acca, accb):
        o = (acca[:LANES] / acca[LANES:LANES + 1] - lam * (accb[:LANES] / accb[LANES:LANES + 1]))
        o = o * lax.rsqrt(jnp.mean(o * o, axis=0, keepdims=True) + EPS)
        o_ref[i * TQ:(i + 1) * TQ, :] = (o.T * sg_ref[...] * (1.0 - lam_init)).astype(o_ref.dtype)

    _attend(work, q_tile, k_tile, vt_tile, finish, sc_scr, p_scr, acc_scr)


def _diff_attn(main3, maint, rel_bias, lq1, lk1, lq2, lk2, subln_g, lam_init):
    b, s, _ = main3.shape
    small = lambda w: pl.BlockSpec((1, w), lambda h, i: (0, 0))
    trow = lambda base: pl.BlockSpec((None, LANES, s), lambda h, i: (i, base // LANES + h, 0))
    return pl.pallas_call(
        functools.partial(_diff_body, lam_init=lam_init),
        grid=(DIFF_HEADS, b),
        in_specs=[
            pl.BlockSpec(memory_space=pltpu.SMEM),
            trow(ROW_DQ),
            pl.BlockSpec((None, s, LANES), lambda h, i: (i, 0, COL_DK // LANES + h)),
            trow(ROW_DV),
            small(DIFF_HEAD_DIM), small(DIFF_HEAD_DIM), small(DIFF_HEAD_DIM), small(DIFF_HEAD_DIM),
            small(2 * DIFF_HEAD_DIM),
        ],
        out_specs=pl.BlockSpec((None, s, LANES), lambda h, i: (i, 0, h)),
        out_shape=jax.ShapeDtypeStruct((b, s, DIFF_WIDTH), BF16),
        scratch_shapes=[
            pltpu.VMEM((3, TK, TQ), F32),
            pltpu.VMEM((LANES, s), BF16),
            pltpu.VMEM((LANES, s), BF16),
            pltpu.VMEM((LANES + BF16_ROWS, s), BF16),
            pltpu.VMEM((2, 2, TK, TQ), F32),
            pltpu.VMEM((2, 2, TK, TQ), BF16),
            pltpu.VMEM((2, LANES + BF16_ROWS, TQ), F32),
        ],
        compiler_params=pltpu.CompilerParams(
            dimension_semantics=("arbitrary", "arbitrary"), vmem_limit_bytes=VMEM_LIMIT),
        name="diff_attn",
    )(rel_bias, maint, main3, maint, lq1, lk1, lq2, lk2, subln_g)


def _fox_body(qt_ref, k_ref, vt_ref, qaugt_ref, kaug_ref, o_ref,
              mask_scr, qa_scr, qb_scr, ka_scr, kb_scr, va_scr, vb_scr, sc_scr, p_scr, acc_scr):
    s = k_ref.shape[0]
    pair = pl.program_id(0)

    @pl.when(pl.program_id(1) == 0)
    def _():
        ki = lax.broadcasted_iota(jnp.int32, (TK, TQ), 0)
        qi = lax.broadcasted_iota(jnp.int32, (TK, TQ), 1)
        for idx in range(TQ // TK):
            mask_scr[idx] = jnp.where(qi >= ki + idx * TK, 0.0, NEG)

    half = FOX_HEAD_DIM
    lo_a, lo_b = half + AUG_GROUP * pair, AUG_GROUP * pair

    def split(x, aug, idx):
        is_a = idx < half
        aug_a = jnp.where((idx >= lo_a) & (idx < lo_a + AUG_SLOTS), aug, 0.0)
        aug_b = jnp.where((idx >= lo_b) & (idx < lo_b + AUG_SLOTS), aug, 0.0)
        return jnp.where(is_a, x, aug_a).astype(BF16), jnp.where(is_a, aug_b, x).astype(BF16)

    rowi = lax.broadcasted_iota(jnp.int32, qt_ref.shape, 0)
    lanei = lax.broadcasted_iota(jnp.int32, k_ref.shape, 1)
    qa_scr[...], qb_scr[...] = split(qt_ref[...].astype(F32), qaugt_ref[...].astype(F32), rowi)
    ka_scr[...], kb_scr[...] = split(k_ref[...].astype(F32), kaug_ref[...].astype(F32), lanei)
    vt = vt_ref[...].astype(F32)
    va_scr[...] = jnp.where(rowi < half, vt, 1.0).astype(BF16)
    vb_scr[...] = jnp.where(rowi < half, 1.0, vt).astype(BF16)

    k_tile = lambda j, w: (ka_scr, kb_scr)[w][pl.ds(j * TK, TK), :]
    vt_tile = lambda j, w: (va_scr, vb_scr)[w][:, pl.ds(j * TK, TK)]
    rows = lax.broadcasted_iota(jnp.int32, (LANES, TQ), 0)
    q_tile = lambda i, w: (qa_scr, qb_scr)[w][:, i * TQ:(i + 1) * TQ]
    work = []
    for i in range(s // TQ):
        j0 = i * (TQ // TK)
        work.append([(j, None, 0) for j in range(j0)]
                    + [(j0 + idx, mask_scr.at[idx], idx * TK) for idx in range(TQ // TK)])

    def finish(i, acca, accb):
        o = jnp.where(rows < half, acca / acca[half:half + 1], accb / accb[0:1])
        o_ref[i * TQ:(i + 1) * TQ, :] = o.T.astype(o_ref.dtype)

    _attend(work, q_tile, k_tile, vt_tile, finish, sc_scr, p_scr, acc_scr)


def _fox_attn(main3, maint, qaugt, kaug):
    b, s, _ = main3.shape
    trow = lambda base: pl.BlockSpec((None, LANES, s), lambda p, i: (i, base // LANES + p, 0))
    tsc = pltpu.VMEM((LANES, s), BF16)
    return pl.pallas_call(
        _fox_body,
        grid=(FOX_HEADS // 2, b),
        in_specs=[
            trow(ROW_FQ),
            pl.BlockSpec((None, s, LANES), lambda p, i: (i, 0, COL_FK // LANES + p)),
            trow(ROW_FV),
            pl.BlockSpec((None, LANES, s), lambda p, i: (i, 0, 0)),
            pl.BlockSpec((None, s, LANES), lambda p, i: (i, 0, 0)),
        ],
        out_specs=pl.BlockSpec((None, s, LANES), lambda p, i: (i, 0, p)),
        out_shape=jax.ShapeDtypeStruct((b, s, FOX_WIDTH), BF16),
        scratch_shapes=[
            pltpu.VMEM((TQ // TK, TK, TQ), F32),
            tsc, tsc, pltpu.VMEM((s, LANES), BF16), pltpu.VMEM((s, LANES), BF16), tsc, tsc,
            pltpu.VMEM((2, 2, TK, TQ), F32),
            pltpu.VMEM((2, 2, TK, TQ), BF16),
            pltpu.VMEM((2, LANES, TQ), F32),
        ],
        compiler_params=pltpu.CompilerParams(
            dimension_semantics=("arbitrary", "arbitrary"), vmem_limit_bytes=VMEM_LIMIT),
        name="fox_attn",
    )(maint, main3, maint, qaugt, kaug)


def _merge_body(x_ref, g0_ref, g1_ref, g2_ref, gb_ref, gr_ref, hr_ref, yd_ref, yf_ref, wr_ref, wd_ref, wf_ref, wo_ref,
                o_ref):
    def branch(g_ref, k, y, w_ref):
        gate = jax.nn.sigmoid(g_ref[...].astype(F32) + gb_ref[k:k + 1, :])
        return gate * jnp.dot(y, w_ref[...], preferred_element_type=F32)

    m = branch(g1_ref, 1, yd_ref[...], wd_ref) + branch(g2_ref, 2, yf_ref[...], wf_ref)
    y_rnn = (jax.nn.gelu(gr_ref[...].astype(F32)) * hr_ref[...].astype(F32)).astype(BF16)
    m = m + branch(g0_ref, 0, y_rnn, wr_ref)
    o_ref[...] = x_ref[...] + jnp.dot(m.astype(BF16), wo_ref[...], preferred_element_type=F32)


def _merge(x2, main2, gate_b, h_rnn, y_diff, y_fox, w_r, w_d, w_f, w_o):
    t, d = x2.shape
    tm = min(TM_MERGE, t)
    rows = lambda w, c=0: pl.BlockSpec((tm, w), lambda i: (i, c))
    whole = lambda a: pl.BlockSpec(a.shape, lambda i: (0, 0))
    g0 = COL_GATES // d
    return pl.pallas_call(
        _merge_body,
        grid=(t // tm,),
        in_specs=[
            rows(d), rows(d, g0), rows(d, g0 + 1), rows(d, g0 + 2), whole(gate_b),
            rows(D_RNN, COL_GRNN // D_RNN), rows(D_RNN), rows(DIFF_WIDTH), rows(FOX_WIDTH),
            whole(w_r), whole(w_d), whole(w_f), whole(w_o),
        ],
        out_specs=rows(d),
        out_shape=jax.ShapeDtypeStruct((t, d), F32),
        compiler_params=pltpu.CompilerParams(dimension_semantics=("arbitrary",), vmem_limit_bytes=VMEM_LIMIT),
        name="merge",
    )(x2, main2, main2, main2, gate_b, main2, h_rnn, y_diff, y_fox, w_r, w_d, w_f, w_o)


def _ffn_body(x_ref, g_ref, wu_ref, cw_ref, cb_ref, wd_ref, gout_ref, o_ref,
              h_scr, acc_scr, act_scr, carry_scr, *, n_tiles, tiles_per_seq, out_norm):
    i = pl.program_id(0)
    tm = x_ref.shape[0]
    part = lambda c, off=0: pl.ds(off + FF_SPLIT[c], FF_SPLIT[c + 1] - FF_SPLIT[c])

    @pl.when(i == 0)
    def _():
        act_scr[...] = jnp.zeros(act_scr.shape, BF16)
        acc_scr[...] = jnp.zeros(acc_scr.shape, F32)

    @pl.when((jnp.minimum(i, n_tiles - 1) % tiles_per_seq) == 0)
    def _():
        carry_scr[...] = jnp.zeros(carry_scr.shape, F32)

    h_scr[...] = _rms(x_ref[...], g_ref[...]).astype(BF16)

    def up(c, off):
        return jnp.dot(h_scr[...], wu_ref[:, part(c, off)], preferred_element_type=F32)

    def activation(us, c):
        out = []
        for u, off in zip(us, (0, D_FF)):
            cols = part(c, off)
            before = carry_scr[:, cols]
            carry_scr[:, cols] = u[tm - SUBLANES:, :]
            out.append(cb_ref[:, cols] + _shift_rows(u, before, 2) * cw_ref[0:1, cols]
                       + _shift_rows(u, before, 1) * cw_ref[1:2, cols] + u * cw_ref[2:3, cols])
        return (jax.nn.gelu(out[0]) * out[1]).astype(BF16)

    down = lambda act, c: jnp.dot(act, wd_ref[part(c), :], preferred_element_type=F32)

    ug0 = up(0, 0)
    done = acc_scr[...] + down(act_scr[...], 1)
    o_ref[...] = _rms(done, gout_ref[...]) if out_norm else done
    uv0 = up(0, D_FF)
    act0 = activation((ug0, uv0), 0)
    ug1 = up(1, 0)
    acc_scr[...] = x_ref[...] + down(act0, 0)
    uv1 = up(1, D_FF)
    act_scr[...] = activation((ug1, uv1), 1)


def _ffn(x2, g, w_up, conv_w, conv_b, w_down, g_out, seq_len, out_norm):
    t, d = x2.shape
    tm = min(TM_FFN, seq_len)
    n_tiles = t // tm
    return pl.pallas_call(
        functools.partial(_ffn_body, n_tiles=n_tiles, tiles_per_seq=seq_len // tm, out_norm=out_norm),
        grid=(n_tiles + 1,),
        in_specs=[
            pl.BlockSpec((tm, d), lambda i: (jnp.minimum(i, n_tiles - 1), 0)),
            _resident(g.shape), _resident(w_up.shape), _resident(conv_w.shape), _resident(conv_b.shape),
            _resident(w_down.shape), _resident(g_out.shape),
        ],
        out_specs=pl.BlockSpec((tm, d), lambda i: (jnp.maximum(i - 1, 0), 0)),
        out_shape=jax.ShapeDtypeStruct((t, d), F32),
        scratch_shapes=[
            pltpu.VMEM((tm, d), BF16),
            pltpu.VMEM((tm, d), F32),
            pltpu.VMEM((tm, FF_SPLIT[2] - FF_SPLIT[1]), BF16),
            pltpu.VMEM((SUBLANES, 2 * D_FF), F32),
        ],
        compiler_params=pltpu.CompilerParams(dimension_semantics=("arbitrary",), vmem_limit_bytes=VMEM_LIMIT),
        name="ffn",
    )(x2, g, w_up, conv_w, conv_b, w_down, g_out)


def _aug_head_of_lane():
    heads = []
    for lane in range(LANES):
        group, slot = (lane % (LANES // 2)) // AUG_GROUP, lane % AUG_GROUP
        if group < FOX_HEADS // 2 and slot < AUG_SLOTS:
            heads.append(2 * group + (1 if lane < LANES // 2 else 0))
        else:
            heads.append(-1)
    return heads


def _prep_w_in(w_in, fox_b_f):
    colscale = jnp.ones((w_in.shape[-1],), F32)
    colscale = colscale.at[W_DQ:W_DQ + DIFF_QK].set(DIFF_HEAD_DIM ** -0.5 * LOG2E)
    colscale = colscale.at[W_FQ:W_FQ + FOX_WIDTH].set(FOX_HEAD_DIM ** -0.5 * LOG2E)
    wb = (w_in * colscale).astype(BF16)
    heads = _aug_head_of_lane()
    idx = jnp.array([max(h, 0) for h in heads], jnp.int32)
    live = jnp.array([1.0 if h >= 0 else 0.0 for h in heads], F32)
    out = []
    for l in range(w_in.shape[0]):
        seg = lambda a, n, l=l: wb[l, :, a:a + n]
        w_row = jnp.concatenate([seg(W_XRNN, D_RNN), seg(W_GRNN, D_RNN), seg(W_DK, DIFF_QK), seg(W_FK, FOX_WIDTH),
                                 seg(W_GATES, N_BRANCH * D_MODEL)], axis=1)
        w_t = jnp.concatenate([seg(W_DQ, DIFF_QK), seg(W_FQ, FOX_WIDTH), seg(W_DV, DIFF_WIDTH),
                               seg(W_FV, FOX_WIDTH)], axis=1).T
        w_f = (w_in[l, :, W_FLOG:W_FLOG + FOX_HEADS][:, idx] * live).astype(BF16)
        fb = (fox_b_f[l][idx] * live).reshape(1, LANES)
        out.append((w_row, w_t, w_f, fb))
    return out


def kernel(x, norm1_g, w_in, rnn_conv_w, rnn_conv_b, rg_w_r, rg_b_r, rg_w_i, rg_b_i, rg_a,
           diff_lq1, diff_lk1, diff_lq2, diff_lk2, diff_subln_g, rel_bias, fox_b_f, gate_b,
           w_br_rnn, w_br_diff, w_br_fox, w_out, norm2_g, ffn_up, ffn_conv_w, ffn_conv_b,
           ffn_down, final_g):
    bsz, s_len, d = x.shape
    t = bsz * s_len
    x2 = x.reshape(t, d)
    row = lambda v: v.reshape(1, -1)
    w_in_parts = _prep_w_in(w_in, fox_b_f)
    rg_w_r, rg_w_i, w_br_rnn, w_br_diff, w_br_fox, w_out, ffn_up, ffn_down = (
        w.astype(BF16) for w in (rg_w_r, rg_w_i, w_br_rnn, w_br_diff, w_br_fox, w_out, ffn_up, ffn_down))
    for l in range(DEPTH):
        w_row, w_t, w_f, fb = w_in_parts[l]
        main2, maint, qaugt, kaug = _inproj(x2, row(norm1_g[l]), w_row, w_t, w_f, fb, bsz, s_len)
        main3 = main2.reshape(bsz, s_len, N_ROWOUT)

        h_rnn = _rnn(main3, rnn_conv_w[l], row(rnn_conv_b[l]), rg_w_r[l], row(rg_b_r[l]),
                     rg_w_i[l], row(rg_b_i[l]), row(rg_a[l]))

        lam_init = 0.8 - 0.6 * math.exp(-0.3 * l)
        y_diff = _diff_attn(main3, maint, rel_bias, row(diff_lq1[l]), row(diff_lk1[l]), row(diff_lq2[l]),
                            row(diff_lk2[l]), row(diff_subln_g[l]), lam_init)

        y_fox = _fox_attn(main3, maint, qaugt, kaug.reshape(bsz, s_len, LANES))

        x2 = _merge(x2, main2, gate_b[l], h_rnn.reshape(t, D_RNN), y_diff.reshape(t, DIFF_WIDTH),
                    y_fox.reshape(t, FOX_WIDTH), w_br_rnn[l], w_br_diff[l], w_br_fox[l], w_out[l])

        x2 = _ffn(x2, row(norm2_g[l]), ffn_up[l], ffn_conv_w[l], row(ffn_conv_b[l]),
                  ffn_down[l], row(final_g), s_len, out_norm=(l == DEPTH - 1))
    return x2.reshape(bsz, s_len, d)
```

```python
import functools
import math

import jax
import jax.numpy as jnp
from jax import lax
from jax.experimental import pallas as pl
from jax.experimental.pallas import tpu as pltpu

F32 = jnp.float32
BF16 = jnp.bfloat16

D_MODEL = 1024
DEPTH = 2
D_RNN = D_MODEL
RNN_BLOCKS = 8
RNN_BLOCK_W = D_RNN // RNN_BLOCKS
RG_LRU_C = 8.0
DIFF_HEADS = 4
DIFF_HEAD_DIM = 64
DIFF_QK = DIFF_HEADS * 2 * DIFF_HEAD_DIM
DIFF_WIDTH = DIFF_HEADS * 2 * DIFF_HEAD_DIM
FOX_HEADS = 8
FOX_HEAD_DIM = 64
FOX_WIDTH = FOX_HEADS * FOX_HEAD_DIM
N_BUCKETS = 32
MAX_EXACT = N_BUCKETS // 2
MAX_DISTANCE = 128
D_FF = ((8 * D_MODEL // 3 + 127) // 128) * 128
N_BRANCH = 3
EPS = 1e-6
LOG2E = 1.4426950408889634
NEG = -1e30

LANES = 128
SUBLANES = 8
BF16_ROWS = 16

W_XRNN = 0
W_GRNN = W_XRNN + D_RNN
W_DQ = W_GRNN + D_RNN
W_DK = W_DQ + DIFF_QK
W_DV = W_DK + DIFF_QK
W_FQ = W_DV + DIFF_WIDTH
W_FK = W_FQ + FOX_WIDTH
W_FV = W_FK + FOX_WIDTH
W_FLOG = W_FV + FOX_WIDTH
W_GATES = W_FLOG + FOX_HEADS

COL_XRNN = 0
COL_GRNN = COL_XRNN + D_RNN
COL_DK = COL_GRNN + D_RNN
COL_FK = COL_DK + DIFF_QK
COL_GATES = COL_FK + FOX_WIDTH
N_ROWOUT = COL_GATES + N_BRANCH * D_MODEL
ROW_DQ = 0
ROW_FQ = ROW_DQ + DIFF_QK
ROW_DV = ROW_FQ + FOX_WIDTH
ROW_FV = ROW_DV + DIFF_WIDTH
N_TOUT = ROW_FV + FOX_WIDTH

TQ = 512
TK = 256
TM_PROJ = 512
TN_PROJ = 1024
TM_MERGE = 512
TM_FFN = 512
MXU_TILE = 256
FF_SPLIT = (0, (D_FF // MXU_TILE + 1) // 2 * MXU_TILE, D_FF)
RNN_STEP_W = D_RNN
VMEM_LIMIT = 56 * 1024 * 1024

AUG_GROUP = 8
AUG_SLOTS = 6
AUG_PARTS = 3


def _softplus(z):
    return jnp.maximum(z, 0.0) + jnp.log1p(jnp.exp(-jnp.abs(z)))


def _rms(x, g):
    return x * lax.rsqrt(jnp.mean(x * x, axis=-1, keepdims=True) + EPS) * g


def _sigmoid(z):
    return 0.5 * jnp.tanh(0.5 * z) + 0.5


def _shift_rows(u, before, k):
    rows, width = u.shape
    groups = rows // SUBLANES
    rot = pltpu.roll(u.reshape(groups, SUBLANES, width), k, 1)
    prev = jnp.concatenate([pltpu.roll(before, k, 0)[None], rot[:-1]], axis=0)
    sub = lax.broadcasted_iota(jnp.int32, rot.shape, 1)
    return jnp.where(sub < k, prev, rot).reshape(rows, width)


def _forget_operands(f_logit, fb, carry_scr, qaugt_ref, kaug_ref):
    tm = f_logit.shape[0]
    c = -_softplus(-(f_logit + fb)) * LOG2E
    row = lax.broadcasted_iota(jnp.int32, c.shape, 0)
    d = 1
    while d < tm:
        c = c + jnp.where(row >= d, pltpu.roll(c, d, 0), 0.0)
        d *= 2
    c = c + carry_scr[0:1, :]
    carry_scr[0:1, :] = c[tm - 1:tm, :]
    hi = c.astype(BF16).astype(F32)
    mid = (c - hi).astype(BF16).astype(F32)
    lo = (c - hi - mid).astype(BF16).astype(F32)
    slot = lax.broadcasted_iota(jnp.int32, c.shape, 1) % AUG_GROUP
    part = jnp.where(slot % AUG_PARTS == 0, hi, jnp.where(slot % AUG_PARTS == 1, mid, lo))
    used = slot < AUG_SLOTS
    qside = slot < AUG_PARTS
    qaugt_ref[...] = jnp.where(used, jnp.where(qside, part, 1.0), 0.0).T.astype(BF16)
    kaug_ref[...] = jnp.where(used, jnp.where(qside, 1.0, -part), 0.0).astype(BF16)


def _inproj_body(x_ref, g_ref, w_ref, wt_ref, wf_ref, fb_ref, out_ref, outt_ref, qaugt_ref, kaug_ref,
                 h_scr, carry_scr, *, tiles_per_seq):
    @pl.when(pl.program_id(0) % tiles_per_seq == 0)
    def _():
        carry_scr[...] = jnp.zeros(carry_scr.shape, F32)

    h_scr[...] = _rms(x_ref[...], g_ref[...]).astype(BF16)
    _forget_operands(jnp.dot(h_scr[...], wf_ref[...], preferred_element_type=F32), fb_ref[...], carry_scr,
                     qaugt_ref, kaug_ref)
    for j in range(w_ref.shape[1] // TN_PROJ):
        cols = slice(j * TN_PROJ, (j + 1) * TN_PROJ)
        out_ref[:, cols] = jnp.dot(h_scr[...], w_ref[:, cols], preferred_element_type=F32).astype(out_ref.dtype)
    for j in range(wt_ref.shape[0] // TN_PROJ):
        rows = slice(j * TN_PROJ, (j + 1) * TN_PROJ)
        outt_ref[rows, :] = lax.dot_general(wt_ref[rows, :], h_scr[...], (((1,), (1,)), ((), ())),
                                            preferred_element_type=F32).astype(outt_ref.dtype)


def _resident(shape):
    return pl.BlockSpec(shape, lambda *_: (0,) * len(shape), pipeline_mode=pl.Buffered(1))


def _inproj(x2, g, w_row, w_t, w_f, fb, bsz, s_len):
    t, d = x2.shape
    tm = min(TM_PROJ, s_len)
    tps = s_len // tm
    transposed = lambda rows: pl.BlockSpec((None, rows, tm), lambda i: (i // tps, 0, i % tps))
    return pl.pallas_call(
        functools.partial(_inproj_body, tiles_per_seq=tps),
        grid=(t // tm,),
        in_specs=[
            pl.BlockSpec((tm, d), lambda i: (i, 0)),
            _resident(g.shape), _resident(w_row.shape), _resident(w_t.shape), _resident(w_f.shape),
            _resident(fb.shape),
        ],
        out_specs=[
            pl.BlockSpec((tm, N_ROWOUT), lambda i: (i, 0)),
            transposed(N_TOUT),
            transposed(LANES),
            pl.BlockSpec((tm, LANES), lambda i: (i, 0)),
        ],
        out_shape=[
            jax.ShapeDtypeStruct((t, N_ROWOUT), BF16),
            jax.ShapeDtypeStruct((bsz, N_TOUT, s_len), BF16),
            jax.ShapeDtypeStruct((bsz, LANES, s_len), BF16),
            jax.ShapeDtypeStruct((t, LANES), BF16),
        ],
        scratch_shapes=[pltpu.VMEM((tm, d), BF16), pltpu.VMEM((SUBLANES, LANES), F32)],
        compiler_params=pltpu.CompilerParams(dimension_semantics=("arbitrary",), vmem_limit_bytes=VMEM_LIMIT),
        name="inproj",
    )(x2, g, w_row, w_t, w_f, fb)


def _rnn_body(xr_ref, cw_ref, cb_ref, wr_ref, br_ref, wi_ref, bi_ref, ap_ref, y_ref, a_scr, u_scr):
    s, width = xr_ref.shape
    nblk = width // RNN_BLOCK_W
    half = pl.program_id(1)
    zeros = jnp.zeros((SUBLANES, RNN_BLOCK_W), F32)
    for n in range(nblk):
        sl = slice(n * RNN_BLOCK_W, (n + 1) * RNN_BLOCK_W)
        x = xr_ref[:, sl].astype(F32)
        xc = (cb_ref[:, sl] + _shift_rows(x, zeros, 3) * cw_ref[0:1, sl] + _shift_rows(x, zeros, 2) * cw_ref[1:2, sl]
              + _shift_rows(x, zeros, 1) * cw_ref[2:3, sl] + x * cw_ref[3:4, sl])
        xb = xc.astype(BF16)
        blk = half * nblk + n
        zr = jnp.dot(xb, wr_ref[blk], preferred_element_type=F32) + br_ref[:, sl]
        gi = _sigmoid(jnp.dot(xb, wi_ref[blk], preferred_element_type=F32) + bi_ref[:, sl])
        k = (-0.5 * RG_LRU_C) * _softplus(-ap_ref[:, sl])
        a = jnp.exp(k * jnp.tanh(0.5 * zr) + k)
        a_scr[:, sl] = a
        y = 1.0 - a * a
        u_scr[:, sl] = jnp.where(y > 0.0, y * lax.rsqrt(y), 0.0) * (gi * xc)

    row8 = lax.broadcasted_iota(jnp.int32, (SUBLANES, width), 0)

    def group(k, hprev):
        off = pl.multiple_of(k * SUBLANES, SUBLANES)
        a = a_scr[pl.ds(off, SUBLANES), :]
        u = u_scr[pl.ds(off, SUBLANES), :]
        for d in (1, 2, 4):
            a_sh = jnp.where(row8 >= d, pltpu.roll(a, d, 0), 1.0)
            u_sh = jnp.where(row8 >= d, pltpu.roll(u, d, 0), 0.0)
            u = a * u_sh + u
            a = a * a_sh
        h = a * hprev + u
        u_scr[pl.ds(off, SUBLANES), :] = h
        return h[SUBLANES - 1:SUBLANES, :]

    lax.fori_loop(0, s // SUBLANES, group, jnp.zeros((1, width), F32), unroll=4)

    y_ref[...] = u_scr[...].astype(y_ref.dtype)


def _rnn(main3, conv_w, conv_b, w_r, b_r, w_i, b_i, a_param):
    b, s, _ = main3.shape
    w = RNN_STEP_W
    nh = D_RNN // w
    vec = lambda k: pl.BlockSpec((k, w), lambda i, j: (0, j))
    full3 = pl.BlockSpec((RNN_BLOCKS, RNN_BLOCK_W, RNN_BLOCK_W), lambda i, j: (0, 0, 0))
    return pl.pallas_call(
        _rnn_body,
        grid=(b, nh),
        in_specs=[
            pl.BlockSpec((None, s, w), lambda i, j: (i, 0, COL_XRNN // w + j)),
            vec(4), vec(1), full3, vec(1), full3, vec(1), vec(1),
        ],
        out_specs=pl.BlockSpec((None, s, w), lambda i, j: (i, 0, j)),
        out_shape=jax.ShapeDtypeStruct((b, s, D_RNN), BF16),
        scratch_shapes=[pltpu.VMEM((s, w), F32), pltpu.VMEM((s, w), F32)],
        compiler_params=pltpu.CompilerParams(
            dimension_semantics=("arbitrary", "arbitrary"), vmem_limit_bytes=VMEM_LIMIT),
        name="rnn",
    )(main3, conv_w, conv_b, w_r, b_r, w_i, b_i, a_param)


def _attend(work, q_tile, k_tile, vt_tile, finish, sc_scr, p_scr, acc_scr):
    maps = range(2)
    tq = sc_scr.shape[-1]
    items = [(i, j, bias, lo, n == 0, n == len(tiles) - 1)
             for i, tiles in enumerate(work) for n, (j, bias, lo) in enumerate(tiles)]
    assert all(lo == 0 for _, _, _, lo, first, _ in items if first)

    def scores(item, slot):
        i, j, bias, lo, _, _ = item
        tops = []
        for w in maps:
            sc = jnp.dot(k_tile(j, w), q_tile(i, w)[:, lo:], preferred_element_type=F32)
            if bias is not None:
                sc = sc + bias[:, lo:]
            sc_scr[slot, w, :, lo:] = sc
            tops.append(jnp.max(sc, axis=0, keepdims=True))
        return tuple(tops)

    def values(item, slot, alphas):
        i, j, _, lo, first, last = item
        for w in maps:
            pv = jnp.dot(vt_tile(j, w), p_scr[slot, w, :, lo:], preferred_element_type=F32)
            acc_scr[w, :, lo:] = pv if first else alphas[w] * acc_scr[w, :, lo:] + pv
        if last:
            finish(i, acc_scr[0], acc_scr[1])

    ms = alphas = None
    tops = scores(items[0], 0)
    for g, item in enumerate(items):
        slot = g % 2
        if g > 0:
            values(items[g - 1], 1 - slot, alphas)
        next_tops = scores(items[g + 1], 1 - slot) if g + 1 < len(items) else None
        lo, first = item[3], item[4]
        if first:
            ms = (jnp.full((1, tq), NEG, F32),) * 2
        new_ms, alphas = [], []
        for w in maps:
            m_old = ms[w][:, lo:]
            m_new = jnp.maximum(m_old, tops[w])
            alphas.append(jnp.exp2(m_old - m_new))
            p_scr[slot, w, :, lo:] = jnp.exp2(sc_scr[slot, w, :, lo:] - m_new).astype(BF16)
            new_ms.append(m_new if lo == 0 else jnp.concatenate([ms[w][:, :lo], m_new], axis=1))
        ms, tops = new_ms, next_tops
    values(items[-1], (len(items) - 1) % 2, alphas)


def _diff_body(rel_ref, qt_ref, k_ref, vt_ref, lq1_ref, lk1_ref, lq2_ref, lk2_ref, sg_ref, o_ref,
               bias_scr, qa_scr, qb_scr, vx_scr, sc_scr, p_scr, acc_scr, *, lam_init):
    s = k_ref.shape[0]
    head = pl.program_id(0)

    @pl.when(pl.program_id(1) == 0)
    def _():
        ki = lax.broadcasted_iota(jnp.int32, (TK, TQ), 0)
        qi = lax.broadcasted_iota(jnp.int32, (TK, TQ), 1)
        far = rel_ref[N_BUCKETS - 1, head]
        for idx in range(3):
            dist = qi - ki + (1 - idx) * TK
            n = jnp.maximum(dist, 0)
            nf = jnp.maximum(n, 1).astype(F32)
            large = MAX_EXACT + (jnp.log(nf / MAX_EXACT) / math.log(MAX_DISTANCE / MAX_EXACT)
                                 * (N_BUCKETS - MAX_EXACT)).astype(jnp.int32)
            large = jnp.minimum(large, N_BUCKETS - 1)
            bucket = jnp.where(n < MAX_EXACT, n, large)
            bias = jnp.zeros((TK, TQ), F32)
            for bk in range(N_BUCKETS):
                bias = jnp.where(bucket == bk, rel_ref[bk, head], bias)
            bias_scr[idx] = jnp.where(dist >= 0, (bias - far) * LOG2E, NEG)

    rowi = lax.broadcasted_iota(jnp.int32, qt_ref.shape, 0)
    qt = qt_ref[...].astype(F32)
    qa_scr[...] = jnp.where(rowi < DIFF_HEAD_DIM, qt, 0.0).astype(BF16)
    qb_scr[...] = jnp.where(rowi < DIFF_HEAD_DIM, 0.0, qt).astype(BF16)
    vx_scr[0:LANES, :] = vt_ref[...]
    vx_scr[LANES:, :] = jnp.ones((BF16_ROWS, s), BF16)

    lam = (jnp.exp(jnp.sum(lq1_ref[...] * lk1_ref[...], axis=-1, keepdims=True))
           - jnp.exp(jnp.sum(lq2_ref[...] * lk2_ref[...], axis=-1, keepdims=True)) + lam_init)

    k_tile = lambda j, w: k_ref[pl.ds(j * TK, TK), :]
    vt_tile = lambda j, w: vx_scr[:, pl.ds(j * TK, TK)]
    q_tile = lambda i, w: (qa_scr, qb_scr)[w][:, i * TQ:(i + 1) * TQ]
    work = []
    for i in range(s // TQ):
        j0 = i * (TQ // TK)
        near = [(j0 - 1 + idx, bias_scr.at[idx], max(idx - 1, 0) * TK) for idx in range(3) if j0 - 1 + idx >= 0]
        work.append([(j, None, 0) for j in range(j0 - 1)] + near)

    def finish(i, acca, accb):
        o = (acca[:LANES] / acca[LANES:LANES + 1] - lam * (accb[:LANES] / accb[LANES:LANES + 1]))
        o = o * lax.rsqrt(jnp.mean(o * o, axis=0, keepdims=True) + EPS)
        o_ref[i * TQ:(i + 1) * TQ, :] = (o.T * sg_ref[...] * (1.0 - lam_init)).astype(o_ref.dtype)

    _attend(work, q_tile, k_tile, vt_tile, finish, sc_scr, p_scr, acc_scr)


def _diff_attn(main3, maint, rel_bias, lq1, lk1, lq2, lk2, subln_g, lam_init):
    b, s, _ = main3.shape
    small = lambda w: pl.BlockSpec((1, w), lambda h, i: (0, 0))
    trow = lambda base: pl.BlockSpec((None, LANES, s), lambda h, i: (i, base // LANES + h, 0))
    return pl.pallas_call(
        functools.partial(_diff_body, lam_init=lam_init),
        grid=(DIFF_HEADS, b),
        in_specs=[
            pl.BlockSpec(memory_space=pltpu.SMEM),
            trow(ROW_DQ),
            pl.BlockSpec((None, s, LANES), lambda h, i: (i, 0, COL_DK // LANES + h)),
            trow(ROW_DV),
            small(DIFF_HEAD_DIM), small(DIFF_HEAD_DIM), small(DIFF_HEAD_DIM), small(DIFF_HEAD_DIM),
            small(2 * DIFF_HEAD_DIM),
        ],
        out_specs=pl.BlockSpec((None, s, LANES), lambda h, i: (i, 0, h)),
        out_shape=jax.ShapeDtypeStruct((b, s, DIFF_WIDTH), BF16),
        scratch_shapes=[
            pltpu.VMEM((3, TK, TQ), F32),
            pltpu.VMEM((LANES, s), BF16),
            pltpu.VMEM((LANES, s), BF16),
            pltpu.VMEM((LANES + BF16_ROWS, s), BF16),
            pltpu.VMEM((2, 2, TK, TQ), F32),
            pltpu.VMEM((2, 2, TK, TQ), BF16),
            pltpu.VMEM((2, LANES + BF16_ROWS, TQ), F32),
        ],
        compiler_params=pltpu.CompilerParams(
            dimension_semantics=("arbitrary", "arbitrary"), vmem_limit_bytes=VMEM_LIMIT),
        name="diff_attn",
    )(rel_bias, maint, main3, maint, lq1, lk1, lq2, lk2, subln_g)


def _fox_body(qt_ref, k_ref, vt_ref, qaugt_ref, kaug_ref, o_ref,
              mask_scr, qa_scr, qb_scr, ka_scr, kb_scr, va_scr, vb_scr, sc_scr, p_scr, acc_scr):
    s = k_ref.shape[0]
    pair = pl.program_id(0)

    @pl.when(pl.program_id(1) == 0)
    def _():
        ki = lax.broadcasted_iota(jnp.int32, (TK, TQ), 0)
        qi = lax.broadcasted_iota(jnp.int32, (TK, TQ), 1)
        for idx in range(TQ // TK):
            mask_scr[idx] = jnp.where(qi >= ki + idx * TK, 0.0, NEG)

    half = FOX_HEAD_DIM
    lo_a, lo_b = half + AUG_GROUP * pair, AUG_GROUP * pair

    def split(x, aug, idx):
        is_a = idx < half
        aug_a = jnp.where((idx >= lo_a) & (idx < lo_a + AUG_SLOTS), aug, 0.0)
        aug_b = jnp.where((idx >= lo_b) & (idx < lo_b + AUG_SLOTS), aug, 0.0)
        return jnp.where(is_a, x, aug_a).astype(BF16), jnp.where(is_a, aug_b, x).astype(BF16)

    rowi = lax.broadcasted_iota(jnp.int32, qt_ref.shape, 0)
    lanei = lax.broadcasted_iota(jnp.int32, k_ref.shape, 1)
    qa_scr[...], qb_scr[...] = split(qt_ref[...].astype(F32), qaugt_ref[...].astype(F32), rowi)
    ka_scr[...], kb_scr[...] = split(k_ref[...].astype(F32), kaug_ref[...].astype(F32), lanei)
    vt = vt_ref[...].astype(F32)
    va_scr[...] = jnp.where(rowi < half, vt, 1.0).astype(BF16)
    vb_scr[...] = jnp.where(rowi < half, 1.0, vt).astype(BF16)

    k_tile = lambda j, w: (ka_scr, kb_scr)[w][pl.ds(j * TK, TK), :]
    vt_tile = lambda j, w: (va_scr, vb_scr)[w][:, pl.ds(j * TK, TK)]
    rows = lax.broadcasted_iota(jnp.int32, (LANES, TQ), 0)
    q_tile = lambda i, w: (qa_scr, qb_scr)[w][:, i * TQ:(i + 1) * TQ]
    work = []
    for i in range(s // TQ):
        j0 = i * (TQ // TK)
        work.append([(j, None, 0) for j in range(j0)]
                    + [(j0 + idx, mask_scr.at[idx], idx * TK) for idx in range(TQ // TK)])

    def finish(i, acca, accb):
        o = jnp.where(rows < half, acca / acca[half:half + 1], accb / accb[0:1])
        o_ref[i * TQ:(i + 1) * TQ, :] = o.T.astype(o_ref.dtype)

    _attend(work, q_tile, k_tile, vt_tile, finish, sc_scr, p_scr, acc_scr)


def _fox_attn(main3, maint, qaugt, kaug):
    b, s, _ = main3.shape
    trow = lambda base: pl.BlockSpec((None, LANES, s), lambda p, i: (i, base // LANES + p, 0))
    tsc = pltpu.VMEM((LANES, s), BF16)
    return pl.pallas_call(
        _fox_body,
        grid=(FOX_HEADS // 2, b),
        in_specs=[
            trow(ROW_FQ),
            pl.BlockSpec((None, s, LANES), lambda p, i: (i, 0, COL_FK // LANES + p)),
            trow(ROW_FV),
            pl.BlockSpec((None, LANES, s), lambda p, i: (i, 0, 0)),
            pl.BlockSpec((None, s, LANES), lambda p, i: (i, 0, 0)),
        ],
        out_specs=pl.BlockSpec((None, s, LANES), lambda p, i: (i, 0, p)),
        out_shape=jax.ShapeDtypeStruct((b, s, FOX_WIDTH), BF16),
        scratch_shapes=[
            pltpu.VMEM((TQ // TK, TK, TQ), F32),
            tsc, tsc, pltpu.VMEM((s, LANES), BF16), pltpu.VMEM((s, LANES), BF16), tsc, tsc,
            pltpu.VMEM((2, 2, TK, TQ), F32),
            pltpu.VMEM((2, 2, TK, TQ), BF16),
            pltpu.VMEM((2, LANES, TQ), F32),
        ],
        compiler_params=pltpu.CompilerParams(
            dimension_semantics=("arbitrary", "arbitrary"), vmem_limit_bytes=VMEM_LIMIT),
        name="fox_attn",
    )(maint, main3, maint, qaugt, kaug)


def _merge_body(x_ref, g0_ref, g1_ref, g2_ref, gb_ref, gr_ref, hr_ref, yd_ref, yf_ref, wr_ref, wd_ref, wf_ref, wo_ref,
                o_ref):
    def branch(g_ref, k, y, w_ref):
        gate = jax.nn.sigmoid(g_ref[...].astype(F32) + gb_ref[k:k + 1, :])
        return gate * jnp.dot(y, w_ref[...], preferred_element_type=F32)

    y_rnn = (jax.nn.gelu(gr_ref[...].astype(F32)) * hr_ref[...].astype(F32)).astype(BF16)
    m = (branch(g0_ref, 0, y_rnn, wr_ref) + branch(g1_ref, 1, yd_ref[...], wd_ref)
         + branch(g2_ref, 2, yf_ref[...], wf_ref))
    o_ref[...] = x_ref[...] + jnp.dot(m.astype(BF16), wo_ref[...], preferred_element_type=F32)


def _merge(x2, main2, gate_b, h_rnn, y_diff, y_fox, w_r, w_d, w_f, w_o):
    t, d = x2.shape
    tm = min(TM_MERGE, t)
    rows = lambda w, c=0: pl.BlockSpec((tm, w), lambda i: (i, c))
    whole = lambda a: pl.BlockSpec(a.shape, lambda i: (0, 0))
    g0 = COL_GATES // d
    return pl.pallas_call(
        _merge_body,
        grid=(t // tm,),
        in_specs=[
            rows(d), rows(d, g0), rows(d, g0 + 1), rows(d, g0 + 2), whole(gate_b),
            rows(D_RNN, COL_GRNN // D_RNN), rows(D_RNN), rows(DIFF_WIDTH), rows(FOX_WIDTH),
            whole(w_r), whole(w_d), whole(w_f), whole(w_o),
        ],
        out_specs=rows(d),
        out_shape=jax.ShapeDtypeStruct((t, d), F32),
        compiler_params=pltpu.CompilerParams(dimension_semantics=("arbitrary",), vmem_limit_bytes=VMEM_LIMIT),
        name="merge",
    )(x2, main2, main2, main2, gate_b, main2, h_rnn, y_diff, y_fox, w_r, w_d, w_f, w_o)


def _ffn_body(x_ref, g_ref, wu_ref, cw_ref, cb_ref, wd_ref, gout_ref, o_ref,
              h_scr, acc_scr, act_scr, carry_scr, *, n_tiles, tiles_per_seq, out_norm):
    i = pl.program_id(0)
    tm = x_ref.shape[0]
    part = lambda c, off=0: pl.ds(off + FF_SPLIT[c], FF_SPLIT[c + 1] - FF_SPLIT[c])

    @pl.when(i == 0)
    def _():
        act_scr[...] = jnp.zeros(act_scr.shape, BF16)
        acc_scr[...] = jnp.zeros(acc_scr.shape, F32)

    @pl.when((jnp.minimum(i, n_tiles - 1) % tiles_per_seq) == 0)
    def _():
        carry_scr[...] = jnp.zeros(carry_scr.shape, F32)

    h_scr[...] = _rms(x_ref[...], g_ref[...]).astype(BF16)

    def up(c, off):
        return jnp.dot(h_scr[...], wu_ref[:, part(c, off)], preferred_element_type=F32)

    def activation(us, c):
        out = []
        for u, off in zip(us, (0, D_FF)):
            cols = part(c, off)
            before = carry_scr[:, cols]
            carry_scr[:, cols] = u[tm - SUBLANES:, :]
            out.append(cb_ref[:, cols] + _shift_rows(u, before, 2) * cw_ref[0:1, cols]
                       + _shift_rows(u, before, 1) * cw_ref[1:2, cols] + u * cw_ref[2:3, cols])
        return (jax.nn.gelu(out[0]) * out[1]).astype(BF16)

    down = lambda act, c: jnp.dot(act, wd_ref[part(c), :], preferred_element_type=F32)

    n_parts = len(FF_SPLIT) - 1
    ug = up(0, 0)
    done = acc_scr[...] + down(act_scr[...], n_parts - 1)
    o_ref[...] = _rms(done, gout_ref[...]) if out_norm else done
    uv = up(0, D_FF)
    acc = x_ref[...]
    for c in range(n_parts):
        act = activation((ug, uv), c)
        if c + 1 < n_parts:
            ug = up(c + 1, 0)
            acc = acc + down(act, c)
            uv = up(c + 1, D_FF)
        else:
            acc_scr[...] = acc
            act_scr[...] = act


def _ffn(x2, g, w_up, conv_w, conv_b, w_down, g_out, seq_len, out_norm):
    t, d = x2.shape
    tm = min(TM_FFN, seq_len)
    n_tiles = t // tm
    return pl.pallas_call(
        functools.partial(_ffn_body, n_tiles=n_tiles, tiles_per_seq=seq_len // tm, out_norm=out_norm),
        grid=(n_tiles + 1,),
        in_specs=[
            pl.BlockSpec((tm, d), lambda i: (jnp.minimum(i, n_tiles - 1), 0)),
            _resident(g.shape), _resident(w_up.shape), _resident(conv_w.shape), _resident(conv_b.shape),
            _resident(w_down.shape), _resident(g_out.shape),
        ],
        out_specs=pl.BlockSpec((tm, d), lambda i: (jnp.maximum(i - 1, 0), 0)),
        out_shape=jax.ShapeDtypeStruct((t, d), F32),
        scratch_shapes=[
            pltpu.VMEM((tm, d), BF16),
            pltpu.VMEM((tm, d), F32),
            pltpu.VMEM((tm, FF_SPLIT[-1] - FF_SPLIT[-2]), BF16),
            pltpu.VMEM((SUBLANES, 2 * D_FF), F32),
        ],
        compiler_params=pltpu.CompilerParams(dimension_semantics=("arbitrary",), vmem_limit_bytes=VMEM_LIMIT),
        name="ffn",
    )(x2, g, w_up, conv_w, conv_b, w_down, g_out)


def _aug_head_of_lane():
    heads = []
    for lane in range(LANES):
        group, slot = (lane % (LANES // 2)) // AUG_GROUP, lane % AUG_GROUP
        if group < FOX_HEADS // 2 and slot < AUG_SLOTS:
            heads.append(2 * group + (1 if lane < LANES // 2 else 0))
        else:
            heads.append(-1)
    return heads


def _prep_w_in(w_in, fox_b_f):
    colscale = jnp.ones((w_in.shape[-1],), F32)
    colscale = colscale.at[W_DQ:W_DQ + DIFF_QK].set(DIFF_HEAD_DIM ** -0.5 * LOG2E)
    colscale = colscale.at[W_FQ:W_FQ + FOX_WIDTH].set(FOX_HEAD_DIM ** -0.5 * LOG2E)
    wb = (w_in * colscale).astype(BF16)
    heads = _aug_head_of_lane()
    idx = jnp.array([max(h, 0) for h in heads], jnp.int32)
    live = jnp.array([1.0 if h >= 0 else 0.0 for h in heads], F32)
    out = []
    for l in range(w_in.shape[0]):
        seg = lambda a, n, l=l: wb[l, :, a:a + n]
        w_row = jnp.concatenate([seg(W_XRNN, D_RNN), seg(W_GRNN, D_RNN), seg(W_DK, DIFF_QK), seg(W_FK, FOX_WIDTH),
                                 seg(W_GATES, N_BRANCH * D_MODEL)], axis=1)
        w_t = jnp.concatenate([seg(W_DQ, DIFF_QK), seg(W_FQ, FOX_WIDTH), seg(W_DV, DIFF_WIDTH),
                               seg(W_FV, FOX_WIDTH)], axis=1).T
        w_f = (w_in[l, :, W_FLOG:W_FLOG + FOX_HEADS][:, idx] * live).astype(BF16)
        fb = (fox_b_f[l][idx] * live).reshape(1, LANES)
        out.append((w_row, w_t, w_f, fb))
    return out


def kernel(x, norm1_g, w_in, rnn_conv_w, rnn_conv_b, rg_w_r, rg_b_r, rg_w_i, rg_b_i, rg_a,
           diff_lq1, diff_lk1, diff_lq2, diff_lk2, diff_subln_g, rel_bias, fox_b_f, gate_b,
           w_br_rnn, w_br_diff, w_br_fox, w_out, norm2_g, ffn_up, ffn_conv_w, ffn_conv_b,
           ffn_down, final_g):
    bsz, s_len, d = x.shape
    t = bsz * s_len
    x2 = x.reshape(t, d)
    row = lambda v: v.reshape(1, -1)
    w_in_parts = _prep_w_in(w_in, fox_b_f)
    rg_w_r, rg_w_i, w_br_rnn, w_br_diff, w_br_fox, w_out, ffn_up, ffn_down = (
        w.astype(BF16) for w in (rg_w_r, rg_w_i, w_br_rnn, w_br_diff, w_br_fox, w_out, ffn_up, ffn_down))
    for l in range(DEPTH):
        w_row, w_t, w_f, fb = w_in_parts[l]
        main2, maint, qaugt, kaug = _inproj(x2, row(norm1_g[l]), w_row, w_t, w_f, fb, bsz, s_len)
        main3 = main2.reshape(bsz, s_len, N_ROWOUT)

        h_rnn = _rnn(main3, rnn_conv_w[l], row(rnn_conv_b[l]), rg_w_r[l], row(rg_b_r[l]),
                     rg_w_i[l], row(rg_b_i[l]), row(rg_a[l]))

        lam_init = 0.8 - 0.6 * math.exp(-0.3 * l)
        y_diff = _diff_attn(main3, maint, rel_bias, row(diff_lq1[l]), row(diff_lk1[l]), row(diff_lq2[l]),
                            row(diff_lk2[l]), row(diff_subln_g[l]), lam_init)

        y_fox = _fox_attn(main3, maint, qaugt, kaug.reshape(bsz, s_len, LANES))

        x2 = _merge(x2, main2, gate_b[l], h_rnn.reshape(t, D_RNN), y_diff.reshape(t, DIFF_WIDTH),
                    y_fox.reshape(t, FOX_WIDTH), w_br_rnn[l], w_br_diff[l], w_br_fox[l], w_out[l])

        x2 = _ffn(x2, row(norm2_g[l]), ffn_up[l], ffn_conv_w[l], row(ffn_conv_b[l]),
                  ffn_down[l], row(final_g), s_len, out_norm=(l == DEPTH - 1))
    return x2.reshape(bsz, s_len, d)
```

```python
import functools
import math

import jax
import jax.numpy as jnp
from jax import lax
from jax.experimental import pallas as pl
from jax.experimental.pallas import tpu as pltpu

F32 = jnp.float32
BF16 = jnp.bfloat16

D_MODEL = 1024
DEPTH = 2
D_RNN = D_MODEL
RNN_BLOCKS = 8
RNN_BLOCK_W = D_RNN // RNN_BLOCKS
RG_LRU_C = 8.0
DIFF_HEADS = 4
DIFF_HEAD_DIM = 64
DIFF_QK = DIFF_HEADS * 2 * DIFF_HEAD_DIM
DIFF_WIDTH = DIFF_HEADS * 2 * DIFF_HEAD_DIM
FOX_HEADS = 8
FOX_HEAD_DIM = 64
FOX_WIDTH = FOX_HEADS * FOX_HEAD_DIM
N_BUCKETS = 32
MAX_EXACT = N_BUCKETS // 2
MAX_DISTANCE = 128
D_FF = ((8 * D_MODEL // 3 + 127) // 128) * 128
N_BRANCH = 3
EPS = 1e-6
LOG2E = 1.4426950408889634
NEG = -1e30

LANES = 128
SUBLANES = 8
BF16_ROWS = 16

W_XRNN = 0
W_GRNN = W_XRNN + D_RNN
W_DQ = W_GRNN + D_RNN
W_DK = W_DQ + DIFF_QK
W_DV = W_DK + DIFF_QK
W_FQ = W_DV + DIFF_WIDTH
W_FK = W_FQ + FOX_WIDTH
W_FV = W_FK + FOX_WIDTH
W_FLOG = W_FV + FOX_WIDTH
W_GATES = W_FLOG + FOX_HEADS

COL_XRNN = 0
COL_GRNN = COL_XRNN + D_RNN
COL_DK = COL_GRNN + D_RNN
COL_FK = COL_DK + DIFF_QK
COL_GATES = COL_FK + FOX_WIDTH
N_ROWOUT = COL_GATES + N_BRANCH * D_MODEL
ROW_DQ = 0
ROW_FQ = ROW_DQ + DIFF_QK
ROW_DV = ROW_FQ + FOX_WIDTH
ROW_FV = ROW_DV + DIFF_WIDTH
N_TOUT = ROW_FV + FOX_WIDTH

TQ = 512
TK = 256
TM_PROJ = 512
TN_PROJ = 1024
TM_MERGE = 512
TM_FFN = 512
MXU_TILE = 256
FF_SPLIT = (0, (D_FF // MXU_TILE + 1) // 2 * MXU_TILE, D_FF)
RNN_STEP_W = D_RNN
VMEM_LIMIT = 56 * 1024 * 1024

AUG_GROUP = 8
AUG_SLOTS = 6
AUG_PARTS = 3


def _softplus(z):
    return jnp.maximum(z, 0.0) + jnp.log1p(jnp.exp(-jnp.abs(z)))


def _rms(x, g):
    return x * lax.rsqrt(jnp.mean(x * x, axis=-1, keepdims=True) + EPS) * g


def _sigmoid(z):
    return 0.5 * jnp.tanh(0.5 * z) + 0.5


def _shift_rows(u, before, k):
    rows, width = u.shape
    groups = rows // SUBLANES
    rot = pltpu.roll(u.reshape(groups, SUBLANES, width), k, 1)
    prev = jnp.concatenate([pltpu.roll(before, k, 0)[None], rot[:-1]], axis=0)
    sub = lax.broadcasted_iota(jnp.int32, rot.shape, 1)
    return jnp.where(sub < k, prev, rot).reshape(rows, width)


def _forget_operands(f_logit, fb, carry_scr, qaugt_ref, kaug_ref):
    tm = f_logit.shape[0]
    c = -_softplus(-(f_logit + fb)) * LOG2E
    row = lax.broadcasted_iota(jnp.int32, c.shape, 0)
    d = 1
    while d < tm:
        c = c + jnp.where(row >= d, pltpu.roll(c, d, 0), 0.0)
        d *= 2
    c = c + carry_scr[0:1, :]
    carry_scr[0:1, :] = c[tm - 1:tm, :]
    hi = c.astype(BF16).astype(F32)
    mid = (c - hi).astype(BF16).astype(F32)
    lo = (c - hi - mid).astype(BF16).astype(F32)
    slot = lax.broadcasted_iota(jnp.int32, c.shape, 1) % AUG_GROUP
    part = jnp.where(slot % AUG_PARTS == 0, hi, jnp.where(slot % AUG_PARTS == 1, mid, lo))
    used = slot < AUG_SLOTS
    qside = slot < AUG_PARTS
    qaugt_ref[...] = jnp.where(used, jnp.where(qside, part, 1.0), 0.0).T.astype(BF16)
    kaug_ref[...] = jnp.where(used, jnp.where(qside, 1.0, -part), 0.0).astype(BF16)


def _inproj_body(x_ref, g_ref, w_ref, wt_ref, wf_ref, fb_ref, out_ref, outt_ref, qaugt_ref, kaug_ref,
                 h_scr, carry_scr, *, tiles_per_seq):
    @pl.when(pl.program_id(0) % tiles_per_seq == 0)
    def _():
        carry_scr[...] = jnp.zeros(carry_scr.shape, F32)

    h_scr[...] = _rms(x_ref[...], g_ref[...]).astype(BF16)
    _forget_operands(jnp.dot(h_scr[...], wf_ref[...], preferred_element_type=F32), fb_ref[...], carry_scr,
                     qaugt_ref, kaug_ref)
    for j in range(w_ref.shape[1] // TN_PROJ):
        cols = slice(j * TN_PROJ, (j + 1) * TN_PROJ)
        out_ref[:, cols] = jnp.dot(h_scr[...], w_ref[:, cols], preferred_element_type=F32).astype(out_ref.dtype)
    for j in range(wt_ref.shape[1] // TN_PROJ):
        rows = slice(j * TN_PROJ, (j + 1) * TN_PROJ)
        outt_ref[rows, :] = lax.dot_general(wt_ref[:, rows], h_scr[...], (((0,), (1,)), ((), ())),
                                            preferred_element_type=F32).astype(outt_ref.dtype)


def _resident(shape):
    return pl.BlockSpec(shape, lambda *_: (0,) * len(shape), pipeline_mode=pl.Buffered(1))


def _inproj(x2, g, w_row, w_t, w_f, fb, bsz, s_len):
    t, d = x2.shape
    tm = min(TM_PROJ, s_len)
    tps = s_len // tm
    transposed = lambda rows: pl.BlockSpec((None, rows, tm), lambda i: (i // tps, 0, i % tps))
    return pl.pallas_call(
        functools.partial(_inproj_body, tiles_per_seq=tps),
        grid=(t // tm,),
        in_specs=[
            pl.BlockSpec((tm, d), lambda i: (i, 0)),
            _resident(g.shape), _resident(w_row.shape), _resident(w_t.shape), _resident(w_f.shape),
            _resident(fb.shape),
        ],
        out_specs=[
            pl.BlockSpec((tm, N_ROWOUT), lambda i: (i, 0)),
            transposed(N_TOUT),
            transposed(LANES),
            pl.BlockSpec((tm, LANES), lambda i: (i, 0)),
        ],
        out_shape=[
            jax.ShapeDtypeStruct((t, N_ROWOUT), BF16),
            jax.ShapeDtypeStruct((bsz, N_TOUT, s_len), BF16),
            jax.ShapeDtypeStruct((bsz, LANES, s_len), BF16),
            jax.ShapeDtypeStruct((t, LANES), BF16),
        ],
        scratch_shapes=[pltpu.VMEM((tm, d), BF16), pltpu.VMEM((SUBLANES, LANES), F32)],
        compiler_params=pltpu.CompilerParams(dimension_semantics=("arbitrary",), vmem_limit_bytes=VMEM_LIMIT),
        name="inproj",
    )(x2, g, w_row, w_t, w_f, fb)


def _rnn_body(xr_ref, cw_ref, cb_ref, wr_ref, br_ref, wi_ref, bi_ref, ap_ref, y_ref, a_scr, u_scr):
    s, width = xr_ref.shape
    nblk = width // RNN_BLOCK_W
    half = pl.program_id(1)
    zeros = jnp.zeros((SUBLANES, RNN_BLOCK_W), F32)
    for n in range(nblk):
        sl = slice(n * RNN_BLOCK_W, (n + 1) * RNN_BLOCK_W)
        x = xr_ref[:, sl].astype(F32)
        xc = (cb_ref[:, sl] + _shift_rows(x, zeros, 3) * cw_ref[0:1, sl] + _shift_rows(x, zeros, 2) * cw_ref[1:2, sl]
              + _shift_rows(x, zeros, 1) * cw_ref[2:3, sl] + x * cw_ref[3:4, sl])
        xb = xc.astype(BF16)
        blk = half * nblk + n
        zr = jnp.dot(xb, wr_ref[blk], preferred_element_type=F32) + br_ref[:, sl]
        gi = _sigmoid(jnp.dot(xb, wi_ref[blk], preferred_element_type=F32) + bi_ref[:, sl])
        k = (-0.5 * RG_LRU_C) * _softplus(-ap_ref[:, sl])
        a = jnp.exp(k * jnp.tanh(0.5 * zr) + k)
        a_scr[:, sl] = a
        y = 1.0 - a * a
        u_scr[:, sl] = jnp.where(y > 0.0, y * lax.rsqrt(y), 0.0) * (gi * xc)

    row8 = lax.broadcasted_iota(jnp.int32, (SUBLANES, width), 0)

    def group(k, hprev):
        off = pl.multiple_of(k * SUBLANES, SUBLANES)
        a = a_scr[pl.ds(off, SUBLANES), :]
        u = u_scr[pl.ds(off, SUBLANES), :]
        for d in (1, 2, 4):
            a_sh = jnp.where(row8 >= d, pltpu.roll(a, d, 0), 1.0)
            u_sh = jnp.where(row8 >= d, pltpu.roll(u, d, 0), 0.0)
            u = a * u_sh + u
            a = a * a_sh
        h = a * hprev + u
        u_scr[pl.ds(off, SUBLANES), :] = h
        return h[SUBLANES - 1:SUBLANES, :]

    lax.fori_loop(0, s // SUBLANES, group, jnp.zeros((1, width), F32), unroll=4)

    y_ref[...] = u_scr[...].astype(y_ref.dtype)


def _rnn(main3, conv_w, conv_b, w_r, b_r, w_i, b_i, a_param):
    b, s, _ = main3.shape
    w = RNN_STEP_W
    nh = D_RNN // w
    vec = lambda k: pl.BlockSpec((k, w), lambda i, j: (0, j))
    full3 = pl.BlockSpec((RNN_BLOCKS, RNN_BLOCK_W, RNN_BLOCK_W), lambda i, j: (0, 0, 0))
    return pl.pallas_call(
        _rnn_body,
        grid=(b, nh),
        in_specs=[
            pl.BlockSpec((None, s, w), lambda i, j: (i, 0, COL_XRNN // w + j)),
            vec(4), vec(1), full3, vec(1), full3, vec(1), vec(1),
        ],
        out_specs=pl.BlockSpec((None, s, w), lambda i, j: (i, 0, j)),
        out_shape=jax.ShapeDtypeStruct((b, s, D_RNN), BF16),
        scratch_shapes=[pltpu.VMEM((s, w), F32), pltpu.VMEM((s, w), F32)],
        compiler_params=pltpu.CompilerParams(
            dimension_semantics=("arbitrary", "arbitrary"), vmem_limit_bytes=VMEM_LIMIT),
        name="rnn",
    )(main3, conv_w, conv_b, w_r, b_r, w_i, b_i, a_param)


def _attend(work, q_tile, k_tile, vt_tile, finish, sc_scr, p_scr, acc_scr):
    maps = range(2)
    tq = sc_scr.shape[-1]
    items = [(i, j, bias, lo, n == 0, n == len(tiles) - 1)
             for i, tiles in enumerate(work) for n, (j, bias, lo) in enumerate(tiles)]
    assert all(lo == 0 for _, _, _, lo, first, _ in items if first)

    def scores(item, slot):
        i, j, bias, lo, _, _ = item
        tops = []
        for w in maps:
            sc = jnp.dot(k_tile(j, w), q_tile(i, w)[:, lo:], preferred_element_type=F32)
            if bias is not None:
                sc = sc + bias[:, lo:]
            sc_scr[slot, w, :, lo:] = sc
            tops.append(jnp.max(sc, axis=0, keepdims=True))
        return tuple(tops)

    def values(item, slot, alphas):
        i, j, _, lo, first, last = item
        for w in maps:
            pv = jnp.dot(vt_tile(j, w), p_scr[slot, w, :, lo:], preferred_element_type=F32)
            acc_scr[w, :, lo:] = pv if first else alphas[w] * acc_scr[w, :, lo:] + pv
        if last:
            finish(i, acc_scr[0], acc_scr[1])

    ms = alphas = None
    tops = scores(items[0], 0)
    for g, item in enumerate(items):
        slot = g % 2
        if g > 0:
            values(items[g - 1], 1 - slot, alphas)
        next_tops = scores(items[g + 1], 1 - slot) if g + 1 < len(items) else None
        lo, first = item[3], item[4]
        if first:
            ms = (jnp.full((1, tq), NEG, F32),) * 2
        new_ms, alphas = [], []
        for w in maps:
            m_old = ms[w][:, lo:]
            m_new = jnp.maximum(m_old, tops[w])
            alphas.append(jnp.exp2(m_old - m_new))
            p_scr[slot, w, :, lo:] = jnp.exp2(sc_scr[slot, w, :, lo:] - m_new).astype(BF16)
            new_ms.append(m_new if lo == 0 else jnp.concatenate([ms[w][:, :lo], m_new], axis=1))
        ms, tops = new_ms, next_tops
    values(items[-1], (len(items) - 1) % 2, alphas)


def _diff_body(rel_ref, qt_ref, k_ref, vt_ref, lq1_ref, lk1_ref, lq2_ref, lk2_ref, sg_ref, o_ref,
               bias_scr, qa_scr, qb_scr, vx_scr, sc_scr, p_scr, acc_scr, *, lam_init):
    s = k_ref.shape[0]
    head = pl.program_id(0)

    @pl.when(pl.program_id(1) == 0)
    def _():
        ki = lax.broadcasted_iota(jnp.int32, (TK, TQ), 0)
        qi = lax.broadcasted_iota(jnp.int32, (TK, TQ), 1)
        far = rel_ref[N_BUCKETS - 1, head]
        for idx in range(3):
            dist = qi - ki + (1 - idx) * TK
            n = jnp.maximum(dist, 0)
            nf = jnp.maximum(n, 1).astype(F32)
            large = MAX_EXACT + (jnp.log(nf / MAX_EXACT) / math.log(MAX_DISTANCE / MAX_EXACT)
                                 * (N_BUCKETS - MAX_EXACT)).astype(jnp.int32)
            large = jnp.minimum(large, N_BUCKETS - 1)
            bucket = jnp.where(n < MAX_EXACT, n, large)
            bias = jnp.zeros((TK, TQ), F32)
            for bk in range(N_BUCKETS):
                bias = jnp.where(bucket == bk, rel_ref[bk, head], bias)
            bias_scr[idx] = jnp.where(dist >= 0, (bias - far) * LOG2E, NEG)

    rowi = lax.broadcasted_iota(jnp.int32, qt_ref.shape, 0)
    qt = qt_ref[...].astype(F32)
    qa_scr[...] = jnp.where(rowi < DIFF_HEAD_DIM, qt, 0.0).astype(BF16)
    qb_scr[...] = jnp.where(rowi < DIFF_HEAD_DIM, 0.0, qt).astype(BF16)
    vx_scr[0:LANES, :] = vt_ref[...]
    vx_scr[LANES:, :] = jnp.ones((BF16_ROWS, s), BF16)

    lam = (jnp.exp(jnp.sum(lq1_ref[...] * lk1_ref[...], axis=-1, keepdims=True))
           - jnp.exp(jnp.sum(lq2_ref[...] * lk2_ref[...], axis=-1, keepdims=True)) + lam_init)

    k_tile = lambda j, w: k_ref[pl.ds(j * TK, TK), :]
    vt_tile = lambda j, w: vx_scr[:, pl.ds(j * TK, TK)]
    q_tile = lambda i, w: (qa_scr, qb_scr)[w][:, i * TQ:(i + 1) * TQ]
    work = []
    for i in range(s // TQ):
        j0 = i * (TQ // TK)
        near = [(j0 - 1 + idx, bias_scr.at[idx], max(idx - 1, 0) * TK) for idx in range(3) if j0 - 1 + idx >= 0]
        work.append([(j, None, 0) for j in range(j0 - 1)] + near)

    def finish(i, acca, accb):
        o = (acca[:LANES] / acca[LANES:LANES + 1] - lam * (accb[:LANES] / accb[LANES:LANES + 1]))
        o = o * lax.rsqrt(jnp.mean(o * o, axis=0, keepdims=True) + EPS)
        o_ref[i * TQ:(i + 1) * TQ, :] = (o.T * sg_ref[...] * (1.0 - lam_init)).astype(o_ref.dtype)

    _attend(work, q_tile, k_tile, vt_tile, finish, sc_scr, p_scr, acc_scr)


def _diff_attn(main3, maint, rel_bias, lq1, lk1, lq2, lk2, subln_g, lam_init):
    b, s, _ = main3.shape
    small = lambda w: pl.BlockSpec((1, w), lambda h, i: (0, 0))
    trow = lambda base: pl.BlockSpec((None, LANES, s), lambda h, i: (i, base // LANES + h, 0))
    return pl.pallas_call(
        functools.partial(_diff_body, lam_init=lam_init),
        grid=(DIFF_HEADS, b),
        in_specs=[
            pl.BlockSpec(memory_space=pltpu.SMEM),
            trow(ROW_DQ),
            pl.BlockSpec((None, s, LANES), lambda h, i: (i, 0, COL_DK // LANES + h)),
            trow(ROW_DV),
            small(DIFF_HEAD_DIM), small(DIFF_HEAD_DIM), small(DIFF_HEAD_DIM), small(DIFF_HEAD_DIM),
            small(2 * DIFF_HEAD_DIM),
        ],
        out_specs=pl.BlockSpec((None, s, LANES), lambda h, i: (i, 0, h)),
        out_shape=jax.ShapeDtypeStruct((b, s, DIFF_WIDTH), BF16),
        scratch_shapes=[
            pltpu.VMEM((3, TK, TQ), F32),
            pltpu.VMEM((LANES, s), BF16),
            pltpu.VMEM((LANES, s), BF16),
            pltpu.VMEM((LANES + BF16_ROWS, s), BF16),
            pltpu.VMEM((2, 2, TK, TQ), F32),
            pltpu.VMEM((2, 2, TK, TQ), BF16),
            pltpu.VMEM((2, LANES + BF16_ROWS, TQ), F32),
        ],
        compiler_params=pltpu.CompilerParams(
            dimension_semantics=("arbitrary", "arbitrary"), vmem_limit_bytes=VMEM_LIMIT),
        name="diff_attn",
    )(rel_bias, maint, main3, maint, lq1, lk1, lq2, lk2, subln_g)


def _fox_body(qt_ref, k_ref, vt_ref, qaugt_ref, kaug_ref, o_ref,
              mask_scr, qa_scr, qb_scr, ka_scr, kb_scr, va_scr, vb_scr, sc_scr, p_scr, acc_scr):
    s = k_ref.shape[0]
    pair = pl.program_id(0)

    @pl.when(pl.program_id(1) == 0)
    def _():
        ki = lax.broadcasted_iota(jnp.int32, (TK, TQ), 0)
        qi = lax.broadcasted_iota(jnp.int32, (TK, TQ), 1)
        for idx in range(TQ // TK):
            mask_scr[idx] = jnp.where(qi >= ki + idx * TK, 0.0, NEG)

    half = FOX_HEAD_DIM
    lo_a, lo_b = half + AUG_GROUP * pair, AUG_GROUP * pair

    def split(x, aug, idx):
        is_a = idx < half
        aug_a = jnp.where((idx >= lo_a) & (idx < lo_a + AUG_SLOTS), aug, 0.0)
        aug_b = jnp.where((idx >= lo_b) & (idx < lo_b + AUG_SLOTS), aug, 0.0)
        return jnp.where(is_a, x, aug_a).astype(BF16), jnp.where(is_a, aug_b, x).astype(BF16)

    rowi = lax.broadcasted_iota(jnp.int32, qt_ref.shape, 0)
    lanei = lax.broadcasted_iota(jnp.int32, k_ref.shape, 1)
    qa_scr[...], qb_scr[...] = split(qt_ref[...].astype(F32), qaugt_ref[...].astype(F32), rowi)
    ka_scr[...], kb_scr[...] = split(k_ref[...].astype(F32), kaug_ref[...].astype(F32), lanei)
    vt = vt_ref[...].astype(F32)
    va_scr[...] = jnp.where(rowi < half, vt, 1.0).astype(BF16)
    vb_scr[...] = jnp.where(rowi < half, 1.0, vt).astype(BF16)

    k_tile = lambda j, w: (ka_scr, kb_scr)[w][pl.ds(j * TK, TK), :]
    vt_tile = lambda j, w: (va_scr, vb_scr)[w][:, pl.ds(j * TK, TK)]
    rows = lax.broadcasted_iota(jnp.int32, (LANES, TQ), 0)
    q_tile = lambda i, w: (qa_scr, qb_scr)[w][:, i * TQ:(i + 1) * TQ]
    work = []
    for i in range(s // TQ):
        j0 = i * (TQ // TK)
        work.append([(j, None, 0) for j in range(j0)]
                    + [(j0 + idx, mask_scr.at[idx], idx * TK) for idx in range(TQ // TK)])

    def finish(i, acca, accb):
        o = jnp.where(rows < half, acca / acca[half:half + 1], accb / accb[0:1])
        o_ref[i * TQ:(i + 1) * TQ, :] = o.T.astype(o_ref.dtype)

    _attend(work, q_tile, k_tile, vt_tile, finish, sc_scr, p_scr, acc_scr)


def _fox_attn(main3, maint, qaugt, kaug):
    b, s, _ = main3.shape
    trow = lambda base: pl.BlockSpec((None, LANES, s), lambda p, i: (i, base // LANES + p, 0))
    tsc = pltpu.VMEM((LANES, s), BF16)
    return pl.pallas_call(
        _fox_body,
        grid=(FOX_HEADS // 2, b),
        in_specs=[
            trow(ROW_FQ),
            pl.BlockSpec((None, s, LANES), lambda p, i: (i, 0, COL_FK // LANES + p)),
            trow(ROW_FV),
            pl.BlockSpec((None, LANES, s), lambda p, i: (i, 0, 0)),
            pl.BlockSpec((None, s, LANES), lambda p, i: (i, 0, 0)),
        ],
        out_specs=pl.BlockSpec((None, s, LANES), lambda p, i: (i, 0, p)),
        out_shape=jax.ShapeDtypeStruct((b, s, FOX_WIDTH), BF16),
        scratch_shapes=[
            pltpu.VMEM((TQ // TK, TK, TQ), F32),
            tsc, tsc, pltpu.VMEM((s, LANES), BF16), pltpu.VMEM((s, LANES), BF16), tsc, tsc,
            pltpu.VMEM((2, 2, TK, TQ), F32),
            pltpu.VMEM((2, 2, TK, TQ), BF16),
            pltpu.VMEM((2, LANES, TQ), F32),
        ],
        compiler_params=pltpu.CompilerParams(
            dimension_semantics=("arbitrary", "arbitrary"), vmem_limit_bytes=VMEM_LIMIT),
        name="fox_attn",
    )(maint, main3, maint, qaugt, kaug)


def _merge_body(x_ref, g0_ref, g1_ref, g2_ref, gb_ref, gr_ref, hr_ref, yd_ref, yf_ref, wr_ref, wd_ref, wf_ref, wo_ref,
                o_ref):
    def branch(g_ref, k, y, w_ref):
        gate = jax.nn.sigmoid(g_ref[...].astype(F32) + gb_ref[k:k + 1, :])
        return gate * jnp.dot(y, w_ref[...], preferred_element_type=F32)

    y_rnn = (jax.nn.gelu(gr_ref[...].astype(F32)) * hr_ref[...].astype(F32)).astype(BF16)
    m = (branch(g0_ref, 0, y_rnn, wr_ref) + branch(g1_ref, 1, yd_ref[...], wd_ref)
         + branch(g2_ref, 2, yf_ref[...], wf_ref))
    o_ref[...] = x_ref[...] + jnp.dot(m.astype(BF16), wo_ref[...], preferred_element_type=F32)


def _merge(x2, main2, gate_b, h_rnn, y_diff, y_fox, w_r, w_d, w_f, w_o):
    t, d = x2.shape
    tm = min(TM_MERGE, t)
    rows = lambda w, c=0: pl.BlockSpec((tm, w), lambda i: (i, c))
    whole = lambda a: pl.BlockSpec(a.shape, lambda i: (0, 0))
    g0 = COL_GATES // d
    return pl.pallas_call(
        _merge_body,
        grid=(t // tm,),
        in_specs=[
            rows(d), rows(d, g0), rows(d, g0 + 1), rows(d, g0 + 2), whole(gate_b),
            rows(D_RNN, COL_GRNN // D_RNN), rows(D_RNN), rows(DIFF_WIDTH), rows(FOX_WIDTH),
            whole(w_r), whole(w_d), whole(w_f), whole(w_o),
        ],
        out_specs=rows(d),
        out_shape=jax.ShapeDtypeStruct((t, d), F32),
        compiler_params=pltpu.CompilerParams(dimension_semantics=("arbitrary",), vmem_limit_bytes=VMEM_LIMIT),
        name="merge",
    )(x2, main2, main2, main2, gate_b, main2, h_rnn, y_diff, y_fox, w_r, w_d, w_f, w_o)


def _ffn_body(x_ref, g_ref, wu_ref, cw_ref, cb_ref, wd_ref, gout_ref, o_ref,
              h_scr, acc_scr, act_scr, carry_scr, *, n_tiles, tiles_per_seq, out_norm):
    i = pl.program_id(0)
    tm = x_ref.shape[0]
    part = lambda c, off=0: pl.ds(off + FF_SPLIT[c], FF_SPLIT[c + 1] - FF_SPLIT[c])

    @pl.when(i == 0)
    def _():
        act_scr[...] = jnp.zeros(act_scr.shape, BF16)
        acc_scr[...] = jnp.zeros(acc_scr.shape, F32)

    @pl.when((jnp.minimum(i, n_tiles - 1) % tiles_per_seq) == 0)
    def _():
        carry_scr[...] = jnp.zeros(carry_scr.shape, F32)

    h_scr[...] = _rms(x_ref[...], g_ref[...]).astype(BF16)

    def up(c, off):
        return jnp.dot(h_scr[...], wu_ref[:, part(c, off)], preferred_element_type=F32)

    def activation(us, c):
        out = []
        for u, off in zip(us, (0, D_FF)):
            cols = part(c, off)
            before = carry_scr[:, cols]
            carry_scr[:, cols] = u[tm - SUBLANES:, :]
            out.append(cb_ref[:, cols] + _shift_rows(u, before, 2) * cw_ref[0:1, cols]
                       + _shift_rows(u, before, 1) * cw_ref[1:2, cols] + u * cw_ref[2:3, cols])
        return (jax.nn.gelu(out[0]) * out[1]).astype(BF16)

    down = lambda act, c: jnp.dot(act, wd_ref[part(c), :], preferred_element_type=F32)

    n_parts = len(FF_SPLIT) - 1
    ug = up(0, 0)
    done = acc_scr[...] + down(act_scr[...], n_parts - 1)
    o_ref[...] = _rms(done, gout_ref[...]) if out_norm else done
    uv = up(0, D_FF)
    acc = x_ref[...]
    for c in range(n_parts):
        act = activation((ug, uv), c)
        if c + 1 < n_parts:
            ug = up(c + 1, 0)
            acc = acc + down(act, c)
            uv = up(c + 1, D_FF)
        else:
            acc_scr[...] = acc
            act_scr[...] = act


def _ffn(x2, g, w_up, conv_w, conv_b, w_down, g_out, seq_len, out_norm):
    t, d = x2.shape
    tm = min(TM_FFN, seq_len)
    n_tiles = t // tm
    return pl.pallas_call(
        functools.partial(_ffn_body, n_tiles=n_tiles, tiles_per_seq=seq_len // tm, out_norm=out_norm),
        grid=(n_tiles + 1,),
        in_specs=[
            pl.BlockSpec((tm, d), lambda i: (jnp.minimum(i, n_tiles - 1), 0)),
            _resident(g.shape), _resident(w_up.shape), _resident(conv_w.shape), _resident(conv_b.shape),
            _resident(w_down.shape), _resident(g_out.shape),
        ],
        out_specs=pl.BlockSpec((tm, d), lambda i: (jnp.maximum(i - 1, 0), 0)),
        out_shape=jax.ShapeDtypeStruct((t, d), F32),
        scratch_shapes=[
            pltpu.VMEM((tm, d), BF16),
            pltpu.VMEM((tm, d), F32),
            pltpu.VMEM((tm, FF_SPLIT[-1] - FF_SPLIT[-2]), BF16),
            pltpu.VMEM((SUBLANES, 2 * D_FF), F32),
        ],
        compiler_params=pltpu.CompilerParams(dimension_semantics=("arbitrary",), vmem_limit_bytes=VMEM_LIMIT),
        name="ffn",
    )(x2, g, w_up, conv_w, conv_b, w_down, g_out)


def _aug_head_of_lane():
    heads = []
    for lane in range(LANES):
        group, slot = (lane % (LANES // 2)) // AUG_GROUP, lane % AUG_GROUP
        if group < FOX_HEADS // 2 and slot < AUG_SLOTS:
            heads.append(2 * group + (1 if lane < LANES // 2 else 0))
        else:
            heads.append(-1)
    return heads


def _prep_w_in(w_in, fox_b_f):
    colscale = jnp.ones((w_in.shape[-1],), F32)
    colscale = colscale.at[W_DQ:W_DQ + DIFF_QK].set(DIFF_HEAD_DIM ** -0.5 * LOG2E)
    colscale = colscale.at[W_FQ:W_FQ + FOX_WIDTH].set(FOX_HEAD_DIM ** -0.5 * LOG2E)
    wb = (w_in * colscale).astype(BF16)
    heads = _aug_head_of_lane()
    idx = jnp.array([max(h, 0) for h in heads], jnp.int32)
    live = jnp.array([1.0 if h >= 0 else 0.0 for h in heads], F32)
    out = []
    for l in range(w_in.shape[0]):
        seg = lambda a, n, l=l: wb[l, :, a:a + n]
        w_row = jnp.concatenate([seg(W_XRNN, D_RNN), seg(W_GRNN, D_RNN), seg(W_DK, DIFF_QK), seg(W_FK, FOX_WIDTH),
                                 seg(W_GATES, N_BRANCH * D_MODEL)], axis=1)
        w_t = jnp.concatenate([seg(W_DQ, DIFF_QK), seg(W_FQ, FOX_WIDTH), seg(W_DV, DIFF_WIDTH),
                               seg(W_FV, FOX_WIDTH)], axis=1)
        w_f = (w_in[l, :, W_FLOG:W_FLOG + FOX_HEADS][:, idx] * live).astype(BF16)
        fb = (fox_b_f[l][idx] * live).reshape(1, LANES)
        out.append((w_row, w_t, w_f, fb))
    return out


def kernel(x, norm1_g, w_in, rnn_conv_w, rnn_conv_b, rg_w_r, rg_b_r, rg_w_i, rg_b_i, rg_a,
           diff_lq1, diff_lk1, diff_lq2, diff_lk2, diff_subln_g, rel_bias, fox_b_f, gate_b,
           w_br_rnn, w_br_diff, w_br_fox, w_out, norm2_g, ffn_up, ffn_conv_w, ffn_conv_b,
           ffn_down, final_g):
    bsz, s_len, d = x.shape
    t = bsz * s_len
    x2 = x.reshape(t, d)
    row = lambda v: v.reshape(1, -1)
    w_in_parts = _prep_w_in(w_in, fox_b_f)
    rg_w_r, rg_w_i, w_br_rnn, w_br_diff, w_br_fox, w_out, ffn_up, ffn_down = (
        w.astype(BF16) for w in (rg_w_r, rg_w_i, w_br_rnn, w_br_diff, w_br_fox, w_out, ffn_up, ffn_down))
    for l in range(DEPTH):
        w_row, w_t, w_f, fb = w_in_parts[l]
        main2, maint, qaugt, kaug = _inproj(x2, row(norm1_g[l]), w_row, w_t, w_f, fb, bsz, s_len)
        main3 = main2.reshape(bsz, s_len, N_ROWOUT)

        h_rnn = _rnn(main3, rnn_conv_w[l], row(rnn_conv_b[l]), rg_w_r[l], row(rg_b_r[l]),
                     rg_w_i[l], row(rg_b_i[l]), row(rg_a[l]))

        lam_init = 0.8 - 0.6 * math.exp(-0.3 * l)
        y_diff = _diff_attn(main3, maint, rel_bias, row(diff_lq1[l]), row(diff_lk1[l]), row(diff_lq2[l]),
                            row(diff_lk2[l]), row(diff_subln_g[l]), lam_init)

        y_fox = _fox_attn(main3, maint, qaugt, kaug.reshape(bsz, s_len, LANES))

        x2 = _merge(x2, main2, gate_b[l], h_rnn.reshape(t, D_RNN), y_diff.reshape(t, DIFF_WIDTH),
                    y_fox.reshape(t, FOX_WIDTH), w_br_rnn[l], w_br_diff[l], w_br_fox[l], w_out[l])

        x2 = _ffn(x2, row(norm2_g[l]), ffn_up[l], ffn_conv_w[l], row(ffn_conv_b[l]),
                  ffn_down[l], row(final_g), s_len, out_norm=(l == DEPTH - 1))
    return x2.reshape(bsz, s_len, d)
```
